```python
import math
import jax, jax.numpy as jnp
from jax import lax
import numpy as np

D_MODEL = 2048
BATCH = 1
SEQ = 16384
DEPTH = 1
DEC_BATCH = 32
DEC_SEQ = 1
PAST_LEN = 16384
PAGE_SIZE = 128

HEAD_DIM = 128
ATTN_WIDTH = D_MODEL // 2
N_HEADS = ATTN_WIDTH // HEAD_DIM
N_IDX_HEADS = 16
IDX_DIM = 64
TOPK_MAX = 256
Q_BLOCK = 128
N_BUCKETS = 32
MAX_DISTANCE = 128
POOL_WIDTH = D_MODEL // 2
POOL_WINDOWS = (2, 4, 8, 16)
POOL_GROUPS = len(POOL_WINDOWS)
POOL_GROUP_DIM = POOL_WIDTH // POOL_GROUPS
POOL_CTX = max(POOL_WINDOWS) - 1
N_EXPERTS = 64
TOP_K = 8
N_GROUP = 8
TOPK_GROUP = 4
EXPERT_DIM = 512
SHARED_DIM = 512
ROUTED_SCALE = 2.5
EXPERT_BLOCK = 128
EPS = 1e-6
IN_WIDTHS = (ATTN_WIDTH, ATTN_WIDTH, ATTN_WIDTH, N_IDX_HEADS * IDX_DIM, IDX_DIM, N_IDX_HEADS, POOL_WIDTH, D_MODEL, D_MODEL)
IN_TOTAL = sum(IN_WIDTHS)
IN_OFFSETS = tuple(sum(IN_WIDTHS[:i + 1]) for i in range(len(IN_WIDTHS) - 1))

kernel_name = 'hybrid_dsa_pool_moe_step'

F32 = jnp.float32


def rms_norm(x, g):
    x32 = x.astype(F32)
    y = x32 * lax.rsqrt(jnp.mean(x32 * x32, axis=-1, keepdims=True) + EPS)
    return (y * g.astype(F32)).astype(x.dtype)


def layer_norm(x, g):
    x32 = x.astype(F32)
    xc = x32 - jnp.mean(x32, axis=-1, keepdims=True)
    y = xc * lax.rsqrt(jnp.mean(xc * xc, axis=-1, keepdims=True) + EPS)
    return (y * g.astype(F32)).astype(x.dtype)


def t5_bucket(rel):
    n = jnp.maximum(rel, 0)
    max_exact = N_BUCKETS // 2
    nf = jnp.maximum(n, 1).astype(F32)
    large = max_exact + (jnp.log(nf / max_exact) / math.log(MAX_DISTANCE / max_exact)
                         * (N_BUCKETS - max_exact)).astype(jnp.int32)
    large = jnp.minimum(large, N_BUCKETS - 1)
    return jnp.where(n < max_exact, n, large)


def adaln_params(c, w_ada, b_ada):
    mod = jax.nn.silu(c) @ w_ada + b_ada
    return [m[:, None, :] for m in jnp.split(mod, 6, axis=-1)]


def mixer_inputs(x, shift, scale, p):
    B, T, _ = x.shape
    h = rms_norm(x, p['g_pre_mix']) * (1 + scale) + shift
    z = h @ p['w_in']
    q, k, v, iq, ik, iw, u, ga, gb = jnp.split(z, IN_OFFSETS, axis=-1)
    q = q.reshape(B, T, N_HEADS, HEAD_DIM)
    k = k.reshape(B, T, N_HEADS, HEAD_DIM)
    v = v.reshape(B, T, N_HEADS, HEAD_DIM)
    iq = iq.reshape(B, T, N_IDX_HEADS, IDX_DIM)
    ik = layer_norm(ik, p['idx_k_norm'])
    return q, k, v, iq, iw, ik, u, ga, gb


def indexer_scores(iq, iw, ik):
    dots = jnp.einsum('bqhd,bsd->bqhs', iq, ik).astype(F32) * (IDX_DIM ** -0.5)
    return jnp.einsum('bqhs,bqh->bqs', jax.nn.relu(dots), iw.astype(F32) * (N_IDX_HEADS ** -0.5))


def sparse_attend(q, kg, vg, rel, valid, rel_bias):
    logits = jnp.einsum('bqhd,bqkhd->bqhk', q, kg).astype(F32) * (HEAD_DIM ** -0.5)
    bias = rel_bias[t5_bucket(rel)].astype(F32)
    logits = logits + jnp.swapaxes(bias, -1, -2)
    logits = jnp.where(valid[:, :, None, :], logits, -jnp.inf)
    probs = jax.nn.softmax(logits, axis=-1).astype(vg.dtype)
    return jnp.einsum('bqhk,bqkhd->bqhd', probs, vg)


def prompt_sparse_attention(q, k, v, iq, iw, ik, rel_bias):
    B, S = q.shape[:2]
    n_sel = min(TOPK_MAX, S // 4)
    key_pos = jnp.arange(S)
    take = jax.vmap(lambda arr, idx: arr[idx])

    def block(i):
        q0 = i * Q_BLOCK
        qb = lax.dynamic_slice_in_dim(q, q0, Q_BLOCK, axis=1)
        iqb = lax.dynamic_slice_in_dim(iq, q0, Q_BLOCK, axis=1)
        iwb = lax.dynamic_slice_in_dim(iw, q0, Q_BLOCK, axis=1)
        qpos = q0 + jnp.arange(Q_BLOCK)
        sc = indexer_scores(iqb, iwb, ik)
        adm = key_pos[None, :] <= qpos[:, None]
        sc = jnp.where(adm[None], sc, -jnp.inf)
        _, sel = lax.top_k(sc, n_sel)
        kg = take(k, sel)
        vg = take(v, sel)
        rel = qpos[None, :, None] - sel
        return sparse_attend(qb, kg, vg, rel, rel >= 0, rel_bias)

    out = lax.map(block, jnp.arange(S // Q_BLOCK))
    return jnp.moveaxis(out, 0, 1).reshape(B, S, ATTN_WIDTH)


def sample_sparse_attention(q, k_new, v_new, iq, iw, ik_new, cache_k, cache_v, cache_idx_k, page_table, rel_bias):
    DB, DS = q.shape[:2]
    P = cache_k.shape[1]
    past = page_table.shape[1] * P
    L = past + DS
    n_sel = min(TOPK_MAX, L // 4)
    take = jax.vmap(lambda arr, idx: arr[idx])
    ik_past = cache_idx_k[page_table].reshape(DB, past, IDX_DIM)
    ik_all = jnp.concatenate([ik_past, ik_new.astype(ik_past.dtype)], axis=1)
    sc = indexer_scores(iq, iw, ik_all)
    qpos = past + jnp.arange(DS)
    adm = jnp.arange(L)[None, :] <= qpos[:, None]
    sc = jnp.where(adm[None], sc, -jnp.inf)
    _, sel = lax.top_k(sc, n_sel)
    in_past = sel < past
    sp = jnp.minimum(sel, past - 1)
    phys = take(page_table, sp // P)
    off = sp % P
    sn = jnp.clip(sel - past, 0, DS - 1)
    kg = jnp.where(in_past[..., None, None], cache_k[phys, off], take(k_new, sn).astype(cache_k.dtype))
    vg = jnp.where(in_past[..., None, None], cache_v[phys, off], take(v_new, sn).astype(cache_v.dtype))
    rel = qpos[None, :, None] - sel
    out = sparse_attend(q, kg, vg, rel, rel >= 0, rel_bias)
    return out.reshape(DB, DS, ATTN_WIDTH)


def multiscale_pool(ext, n_ctx, pos0, w_pool, pool_scale):
    B, N, _ = ext.shape
    T = N - n_ctx
    e = ext.reshape(B, N, POOL_GROUPS, POOL_GROUP_DIM).astype(F32)
    cs = jnp.concatenate([jnp.zeros_like(e[:, :1]), jnp.cumsum(e, axis=1)], axis=1)
    idx = jnp.arange(n_ctx, N)
    apos = pos0 + idx
    outs = []
    for g, w in enumerate(POOL_WINDOWS):
        lo = jnp.maximum(idx + 1 - w, 0)
        win = cs[:, idx + 1, g] - cs[:, lo, g]
        cnt = jnp.minimum(apos + 1, w).astype(F32)
        outs.append(win / cnt[None, :, None] - e[:, idx, g])
    pooled = jnp.stack(outs, axis=2).astype(ext.dtype)
    mixed = jnp.einsum('btgc,gcd->btgd', pooled, w_pool)
    return mixed.reshape(B, T, POOL_WIDTH) * pool_scale


def mixer_output(x, attn, pooled, ga, gb, gate, p):
    m = jax.nn.sigmoid(ga) * (attn @ p['w_proj_attn']) + jax.nn.sigmoid(gb) * (pooled @ p['w_proj_pool'])
    return x + gate * rms_norm(m @ p['w_out'], p['g_post_mix'])


def grouped_experts(h, esel, wsel, w_gate, w_up, w_down):
    T, D = h.shape
    A = T * TOP_K
    eb = min(EXPERT_BLOCK, max(8, A // N_EXPERTS))
    n_blocks = -(-A // eb) + N_EXPERTS
    e_flat = esel.reshape(A)
    tok_flat = jnp.repeat(jnp.arange(T, dtype=jnp.int32), TOP_K)
    w_flat = wsel.reshape(A)
    order = jnp.argsort(e_flat)
    e_s = e_flat[order]
    tok_s = tok_flat[order]
    w_s = w_flat[order]
    counts = jnp.bincount(e_flat, length=N_EXPERTS)
    start = jnp.cumsum(counts) - counts
    padded = (counts + eb - 1) // eb * eb
    pend = jnp.cumsum(padded)
    pstart = pend - padded
    slot = pstart[e_s] + (jnp.arange(A) - start[e_s])
    slot_tok = jnp.zeros((n_blocks * eb,), jnp.int32).at[slot].set(tok_s)
    slot_w = jnp.zeros((n_blocks * eb,), h.dtype).at[slot].set(w_s)
    block_e = jnp.minimum(jnp.searchsorted(pend, jnp.arange(n_blocks) * eb, side='right'), N_EXPERTS - 1)

    def run_block(args):
        tok, e = args
        xb = h[tok]
        return (jax.nn.silu(xb @ w_gate[e]) * (xb @ w_up[e])) @ w_down[e]

    out = lax.map(run_block, (slot_tok.reshape(n_blocks, eb), block_e))
    return jnp.zeros_like(h).at[slot_tok].add(out.reshape(-1, D) * slot_w[:, None])


def moe(h, p):
    T, D = h.shape
    s = jax.nn.sigmoid((h @ p['w_router']).astype(F32))
    sb = s + p['b_router'].astype(F32)
    per_group = N_EXPERTS // N_GROUP
    gscore = lax.top_k(sb.reshape(T, N_GROUP, per_group), 2)[0].sum(-1)
    _, gsel = lax.top_k(gscore, TOPK_GROUP)
    gmask = jnp.any(gsel[:, :, None] == jnp.arange(N_GROUP)[None, None, :], axis=1)
    emask = jnp.repeat(gmask, per_group, axis=1)
    _, esel = lax.top_k(jnp.where(emask, sb, -jnp.inf), TOP_K)
    wsel = jnp.take_along_axis(s, esel, axis=1)
    wsel = wsel / jnp.sum(wsel, axis=-1, keepdims=True) * ROUTED_SCALE
    routed = grouped_experts(h, esel, wsel.astype(h.dtype), p['w_gate_e'], p['w_up_e'], p['w_down_e'])
    shared = (jax.nn.silu(h @ p['w_gate_s']) * (h @ p['w_up_s'])) @ p['w_down_s']
    return routed + shared


def ffn_sublayer(x, shift, scale, gate, p):
    B, T, D = x.shape
    h = rms_norm(x, p['g_pre_ffn']) * (1 + scale) + shift
    y = moe(h.reshape(B * T, D), p).reshape(B, T, D)
    return x + gate * rms_norm(y, p['g_post_ffn'])


def prompt_layer(x, c, p, rel_bias):
    sh1, sc1, g1, sh2, sc2, g2 = adaln_params(c, p['w_ada'], p['b_ada'])
    q, k, v, iq, iw, ik, u, ga, gb = mixer_inputs(x, sh1, sc1, p)
    attn = prompt_sparse_attention(q, k, v, iq, iw, ik, rel_bias)
    pooled = multiscale_pool(u, 0, 0, p['w_pool'], p['pool_scale'])
    x = mixer_output(x, attn, pooled, ga, gb, g1, p)
    x = ffn_sublayer(x, sh2, sc2, g2, p)
    return x, k, v, ik, u[:, -POOL_CTX:]


def sample_layer(x, c, cache_k, cache_v, cache_idx_k, state_pool, page_table, p, rel_bias):
    sh1, sc1, g1, sh2, sc2, g2 = adaln_params(c, p['w_ada'], p['b_ada'])
    q, k, v, iq, iw, ik, u, ga, gb = mixer_inputs(x, sh1, sc1, p)
    attn = sample_sparse_attention(q, k, v, iq, iw, ik, cache_k, cache_v, cache_idx_k, page_table, rel_bias)
    past = page_table.shape[1] * cache_k.shape[1]
    ext = jnp.concatenate([state_pool.astype(u.dtype), u], axis=1)
    pooled = multiscale_pool(ext, POOL_CTX, past - POOL_CTX, p['w_pool'], p['pool_scale'])
    x = mixer_output(x, attn, pooled, ga, gb, g1, p)
    x = ffn_sublayer(x, sh2, sc2, g2, p)
    return x, k, v, ik, ext[:, -POOL_CTX:]


def setup_inputs(seed: int = 0) -> dict:
    key = jax.random.key(seed)
    ks = jax.random.split(key, 40)
    n_pages = PAST_LEN // PAGE_SIZE
    n_phys = (5 * DEC_BATCH * n_pages + 3) // 4

    def nrm(k, shape, scale):
        return jax.random.normal(k, shape, F32) * scale

    def gain(k, shape):
        return 1.0 + 0.05 * jax.random.normal(k, shape, F32)

    page_table = jax.random.permutation(ks[9], n_phys)[:DEC_BATCH * n_pages].reshape(DEC_BATCH, n_pages).astype(jnp.int32)
    return dict(
        x_prompt=nrm(ks[0], (BATCH, SEQ, D_MODEL), 1.0),
        x_sample=nrm(ks[1], (DEC_BATCH, DEC_SEQ, D_MODEL), 1.0),
        c_prompt=nrm(ks[2], (BATCH, D_MODEL), 1.0),
        c_sample=nrm(ks[3], (DEC_BATCH, D_MODEL), 1.0),
        cache_k=nrm(ks[4], (DEPTH, n_phys, PAGE_SIZE, N_HEADS, HEAD_DIM), 1.0),
        cache_v=nrm(ks[5], (DEPTH, n_phys, PAGE_SIZE, N_HEADS, HEAD_DIM), 1.0),
        cache_idx_k=nrm(ks[6], (DEPTH, n_phys, PAGE_SIZE, IDX_DIM), 1.0),
        state_pool=nrm(ks[7], (DEPTH, DEC_BATCH, POOL_CTX, POOL_WIDTH), 1.0),
        page_table=page_table,
        w_ada=nrm(ks[10], (DEPTH, D_MODEL, 6 * D_MODEL), 0.5 * D_MODEL ** -0.5),
        b_ada=nrm(ks[11], (DEPTH, 6 * D_MODEL), 0.02),
        g_pre_mix=gain(ks[12], (DEPTH, D_MODEL)),
        w_in=nrm(ks[13], (DEPTH, D_MODEL, IN_TOTAL), D_MODEL ** -0.5),
        idx_k_norm=gain(ks[14], (DEPTH, IDX_DIM)),
        rel_bias=nrm(ks[15], (N_BUCKETS, N_HEADS), 0.5),
        w_pool=nrm(ks[16], (DEPTH, POOL_GROUPS, POOL_GROUP_DIM, POOL_GROUP_DIM), POOL_GROUP_DIM ** -0.5),
        pool_scale=1.0 + 0.1 * jax.random.normal(ks[17], (DEPTH, POOL_WIDTH), F32),
        w_proj_attn=nrm(ks[18], (DEPTH, ATTN_WIDTH, D_MODEL), ATTN_WIDTH ** -0.5),
        w_proj_pool=nrm(ks[19], (DEPTH, POOL_WIDTH, D_MODEL), POOL_WIDTH ** -0.5),
        w_out=nrm(ks[20], (DEPTH, D_MODEL, D_MODEL), D_MODEL ** -0.5),
        g_post_mix=gain(ks[21], (DEPTH, D_MODEL)),
        g_pre_ffn=gain(ks[22], (DEPTH, D_MODEL)),
        w_router=nrm(ks[23], (DEPTH, D_MODEL, N_EXPERTS), D_MODEL ** -0.5),
        b_router=nrm(ks[24], (DEPTH, N_EXPERTS), 0.01),
        w_gate_e=nrm(ks[25], (DEPTH, N_EXPERTS, D_MODEL, EXPERT_DIM), D_MODEL ** -0.5),
        w_up_e=nrm(ks[26], (DEPTH, N_EXPERTS, D_MODEL, EXPERT_DIM), D_MODEL ** -0.5),
        w_down_e=nrm(ks[27], (DEPTH, N_EXPERTS, EXPERT_DIM, D_MODEL), EXPERT_DIM ** -0.5),
        w_gate_s=nrm(ks[28], (DEPTH, D_MODEL, SHARED_DIM), D_MODEL ** -0.5),
        w_up_s=nrm(ks[29], (DEPTH, D_MODEL, SHARED_DIM), D_MODEL ** -0.5),
        w_down_s=nrm(ks[30], (DEPTH, SHARED_DIM, D_MODEL), SHARED_DIM ** -0.5),
        g_post_ffn=gain(ks[31], (DEPTH, D_MODEL)),
    )


def reference(x_prompt, x_sample, c_prompt, c_sample, cache_k, cache_v, cache_idx_k, state_pool, page_table,
              w_ada, b_ada, g_pre_mix, w_in, idx_k_norm, rel_bias, w_pool, pool_scale, w_proj_attn, w_proj_pool,
              w_out, g_post_mix, g_pre_ffn, w_router, b_router, w_gate_e, w_up_e, w_down_e, w_gate_s, w_up_s,
              w_down_s, g_post_ffn):
    xp, xs = x_prompt, x_sample
    kp_l, vp_l, ikp_l, pp_l = [], [], [], []
    ks_l, vs_l, iks_l, ps_l = [], [], [], []
    for l in range(DEPTH):
        p = dict(w_ada=w_ada[l], b_ada=b_ada[l], g_pre_mix=g_pre_mix[l], w_in=w_in[l], idx_k_norm=idx_k_norm[l],
                 w_pool=w_pool[l], pool_scale=pool_scale[l], w_proj_attn=w_proj_attn[l],
                 w_proj_pool=w_proj_pool[l], w_out=w_out[l], g_post_mix=g_post_mix[l], g_pre_ffn=g_pre_ffn[l],
                 w_router=w_router[l], b_router=b_router[l], w_gate_e=w_gate_e[l], w_up_e=w_up_e[l],
                 w_down_e=w_down_e[l], w_gate_s=w_gate_s[l], w_up_s=w_up_s[l], w_down_s=w_down_s[l],
                 g_post_ffn=g_post_ffn[l])
        xp, kp, vp, ikp, pp = prompt_layer(xp, c_prompt, p, rel_bias)
        xs, kn, vn, ikn, ps = sample_layer(xs, c_sample, cache_k[l], cache_v[l], cache_idx_k[l], state_pool[l],
                                           page_table, p, rel_bias)
        kp_l.append(kp); vp_l.append(vp); ikp_l.append(ikp); pp_l.append(pp)
        ks_l.append(kn); vs_l.append(vn); iks_l.append(ikn); ps_l.append(ps)
    y_prompt = xp
    y_sample = xs
    k_prompt = jnp.stack(kp_l)
    v_prompt = jnp.stack(vp_l)
    idxk_prompt = jnp.stack(ikp_l)
    pool_prompt = jnp.stack(pp_l)
    k_sample = jnp.stack(ks_l)
    v_sample = jnp.stack(vs_l)
    idxk_sample = jnp.stack(iks_l)
    pool_sample = jnp.stack(ps_l)
    return (y_prompt, y_sample, k_prompt, v_prompt, idxk_prompt, pool_prompt, k_sample, v_sample, idxk_sample, pool_sample)
```

```python
import functools
import math

import jax
import jax.numpy as jnp
import numpy as np
from jax import lax
from jax.experimental import pallas as pl
from jax.experimental.pallas import tpu as pltpu

F32 = jnp.float32
BF16 = jnp.bfloat16
I32 = jnp.int32

HEAD_DIM = 128
N_IDX_HEADS = 16
IDX_DIM = 64
TOPK_MAX = 256
N_BUCKETS = 32
MAX_DISTANCE = 128
POOL_WINDOWS = (2, 4, 8, 16)
POOL_CTX = max(POOL_WINDOWS) - 1
N_GROUP = 8
TOPK_GROUP = 4
TOP_K = 8
ROUTED_SCALE = 2.5
EPS = 1e-6

LANES = 128
SUBLANES = 8
VMEM_LIMIT_BYTES = 56 * 1024 * 1024

NEG_BIG = -1e30
NEG_INF_KEY = int(np.int32(np.uint32(0xFF800000)) ^ np.int32(0x7FFFFFFF))
INT_MIN = -(2 ** 31)


def _params(*sem):
    return pltpu.CompilerParams(dimension_semantics=sem, vmem_limit_bytes=VMEM_LIMIT_BYTES)


def _tile(n, pref):
    if n <= pref:
        return n
    t = pref
    while n % t:
        t //= 2
    return t


def _rms(x, g):
    return x * lax.rsqrt(jnp.mean(x * x, axis=-1, keepdims=True) + EPS) * g


def _sigmoid(x):
    return jax.nn.sigmoid(x)


def _ada_kernel(c_ref, w_ref, b_ref, o_ref):
    c = c_ref[...]
    a = (c * _sigmoid(c)).astype(BF16)
    o_ref[...] = jnp.dot(a, w_ref[...].astype(BF16), preferred_element_type=F32) + b_ref[...]


def _ada(c, w, b):
    r, d = c.shape
    n = w.shape[1]
    tn = _tile(n, 1024)
    return pl.pallas_call(
        _ada_kernel,
        grid=(n // tn,),
        in_specs=[pl.BlockSpec((r, d), lambda j: (0, 0)),
                  pl.BlockSpec((d, tn), lambda j: (0, j)),
                  pl.BlockSpec((1, tn), lambda j: (0, j))],
        out_specs=pl.BlockSpec((r, tn), lambda j: (0, j)),
        out_shape=jax.ShapeDtypeStruct((r, n), F32),
        compiler_params=_params("arbitrary"),
        name="ada",
    )(c, w, b)


def _normmod_kernel(x_ref, g_ref, sc_ref, sh_ref, o_ref):
    y = _rms(x_ref[...], g_ref[...])
    o_ref[...] = (y * (1 + sc_ref[...]) + sh_ref[...]).astype(o_ref.dtype)


def _mod_spec(mod, tm, d):
    if mod.shape[0] == 1:
        return pl.BlockSpec((1, d), lambda i, *_: (0, 0))
    return pl.BlockSpec((tm, d), lambda i, *_: (i, 0))


def _normmod(x, g, scale, shift):
    t, d = x.shape
    tm = _tile(t, 512)
    return pl.pallas_call(
        _normmod_kernel,
        grid=(t // tm,),
        in_specs=[pl.BlockSpec((tm, d), lambda i: (i, 0)),
                  pl.BlockSpec((1, d), lambda i: (0, 0)),
                  _mod_spec(scale, tm, d), _mod_spec(shift, tm, d)],
        out_specs=pl.BlockSpec((tm, d), lambda i: (i, 0)),
        out_shape=jax.ShapeDtypeStruct((t, d), BF16),
        compiler_params=_params("arbitrary"),
        name="normmod",
    )(x, g, scale, shift)


def _mm_kernel(a_ref, w_ref, o_ref):
    o_ref[...] = jnp.dot(a_ref[...], w_ref[...], preferred_element_type=F32).astype(o_ref.dtype)


def _mm(a, w, out_dtype):
    m, k = a.shape
    n = w.shape[1]
    tm = _tile(m, 512)
    tn = _tile(n, 512)
    return pl.pallas_call(
        _mm_kernel,
        grid=(m // tm, n // tn),
        in_specs=[pl.BlockSpec((tm, k), lambda i, j: (i, 0)),
                  pl.BlockSpec((k, tn), lambda i, j: (0, j))],
        out_specs=pl.BlockSpec((tm, tn), lambda i, j: (i, j)),
        out_shape=jax.ShapeDtypeStruct((m, n), out_dtype),
        compiler_params=_params("arbitrary", "arbitrary"),
        name="mm",
    )(a, w)


def _idxproj_kernel(a_ref, w_ref, g_ref, ik_ref, iw_ref):
    z = jnp.dot(a_ref[...], w_ref[...], preferred_element_type=F32)
    ik = z[:, :IDX_DIM]
    xc = ik - jnp.mean(ik, axis=-1, keepdims=True)
    y = xc * lax.rsqrt(jnp.mean(xc * xc, axis=-1, keepdims=True) + EPS)
    ik_ref[...] = y * g_ref[...]
    iw_ref[...] = z[:, IDX_DIM:IDX_DIM + N_IDX_HEADS] * (IDX_DIM ** -0.5 * N_IDX_HEADS ** -0.5)


def _idxproj(a, w, g):
    m, k = a.shape
    tm = _tile(m, 512)
    return pl.pallas_call(
        _idxproj_kernel,
        grid=(m // tm,),
        in_specs=[pl.BlockSpec((tm, k), lambda i: (i, 0)),
                  pl.BlockSpec((k, LANES), lambda i: (0, 0)),
                  pl.BlockSpec((1, IDX_DIM), lambda i: (0, 0))],
        out_specs=[pl.BlockSpec((tm, IDX_DIM), lambda i: (i, 0)),
                   pl.BlockSpec((tm, N_IDX_HEADS), lambda i: (i, 0))],
        out_shape=[jax.ShapeDtypeStruct((m, IDX_DIM), F32),
                   jax.ShapeDtypeStruct((m, N_IDX_HEADS), F32)],
        compiler_params=_params("arbitrary"),
        name="idxproj",
    )(a, w, g)


def _bias_kernel(rb_ref, o_ref):
    n_heads = o_ref.shape[1]
    r = lax.broadcasted_iota(I32, (LANES, LANES), 0)
    c = lax.broadcasted_iota(I32, (LANES, LANES), 1)
    max_exact = N_BUCKETS // 2
    for d in range(2):
        n = jnp.maximum(d * LANES + r - c, 0)
        nf = jnp.maximum(n, 1).astype(F32)
        large = max_exact + (jnp.log(nf / max_exact) / math.log(MAX_DISTANCE / max_exact)
                             * (N_BUCKETS - max_exact)).astype(I32)
        large = jnp.minimum(large, N_BUCKETS - 1)
        bucket = jnp.where(n < max_exact, n, large)
        for h in range(n_heads):
            val = jnp.zeros((LANES, LANES), F32)
            for b in range(N_BUCKETS):
                val = jnp.where(bucket == b, rb_ref[b, h], val)
            o_ref[d, h] = val


def _bias_tiles(rel_bias):
    n_heads = rel_bias.shape[1]
    return pl.pallas_call(
        _bias_kernel,
        in_specs=[pl.BlockSpec(memory_space=pltpu.SMEM)],
        out_shape=jax.ShapeDtypeStruct((2, n_heads, LANES, LANES), F32),
        name="bias_tiles",
    )(rel_bias)


def _sort_key(x):
    bits = pltpu.bitcast(x, I32)
    return jnp.where(bits < 0, bits ^ 0x7FFFFFFF, bits)


def _bisect_threshold(count_ge, rows, n_sel):
    def body(it, prefix):
        bit = 31 - it
        cand = prefix + jnp.left_shift(jnp.int32(1), bit)
        cnt = count_ge(cand)
        return jnp.where(cnt >= n_sel, cand, prefix)

    return lax.fori_loop(0, 32, body, jnp.full((rows, LANES), INT_MIN, I32))


def _idx_kernel(iq_ref, ikt_ref, iw_ref, mask_ref, key_ref, wb_ref, *, tq, tk, n_sel):
    i = pl.program_id(0)
    s_total = ikt_ref.shape[1]
    q0 = i * tq
    nc = (q0 + tq) // tk
    nlane = tk // LANES

    for h in range(N_IDX_HEADS):
        wb_ref[h] = jnp.broadcast_to(iw_ref[:, h:h + 1], (tq, LANES))

    def score_chunk(c, carry):
        k0 = pl.multiple_of(c * tk, tk)
        kt = ikt_ref[:, pl.ds(k0, tk)]
        acc = jnp.zeros((tq, tk), F32)
        for h in range(N_IDX_HEADS):
            d = jnp.dot(iq_ref[:, h * IDX_DIM:(h + 1) * IDX_DIM], kt, preferred_element_type=F32)
            w = jnp.concatenate([wb_ref[h]] * nlane, axis=1)
            acc = acc + jnp.maximum(d, 0.0) * w
        acc = acc + 0.0
        t = q0 + lax.broadcasted_iota(I32, (tq, tk), 0)
        s = k0 + lax.broadcasted_iota(I32, (tq, tk), 1)
        acc = jnp.where(s <= t, acc, -jnp.inf)
        key_ref[:, pl.ds(k0, tk)] = _sort_key(acc)
        return carry

    lax.fori_loop(0, nc, score_chunk, 0)

    def count_ge(cand):
        def body(c, cnt):
            k0 = pl.multiple_of(c * tk, tk)
            kk = key_ref[:, pl.ds(k0, tk)]
            for j in range(nlane):
                cnt = cnt + jnp.where(kk[:, j * LANES:(j + 1) * LANES] >= cand, 1, 0)
            return cnt
        cnt = lax.fori_loop(0, nc, body, jnp.zeros((tq, LANES), I32))
        return jnp.sum(cnt.astype(F32), axis=1, keepdims=True)

    thr = _bisect_threshold(count_ge, tq, n_sel)
    is_neg = thr == NEG_INF_KEY
    thr_eff = jnp.where(is_neg, NEG_INF_KEY + 1, thr)
    cge = count_ge(thr)
    tied = jnp.logical_and(jnp.logical_not(is_neg), cge > n_sel)
    any_tied = jnp.max(jnp.where(tied, 1.0, 0.0)) > 0.0

    @pl.when(jnp.logical_not(any_tied))
    def _():
        def body(c, carry):
            k0 = pl.multiple_of(c * tk, tk)
            kk = key_ref[:, pl.ds(k0, tk)]
            thr_t = jnp.concatenate([thr_eff] * nlane, axis=1)
            mask_ref[:, pl.ds(k0, tk)] = jnp.where(kk >= thr_t, 1, 0).astype(jnp.int8)
            return carry
        lax.fori_loop(0, nc, body, 0)

    @pl.when(any_tied)
    def _():
        cgt = count_ge(thr + 1)
        need = jnp.where(is_neg[:, :1], 0.0, n_sel - cgt)
        upper = (lax.broadcasted_iota(I32, (tk, tk), 0)
                 < lax.broadcasted_iota(I32, (tk, tk), 1)).astype(BF16)

        def body(c, carry):
            k0 = pl.multiple_of(c * tk, tk)
            kk = key_ref[:, pl.ds(k0, tk)]
            thr_t = jnp.concatenate([thr] * nlane, axis=1)
            eq = kk == thr_t
            eqf = jnp.where(eq, 1.0, 0.0)
            before = jnp.dot(eqf.astype(BF16), upper, preferred_element_type=F32) + carry
            keep = jnp.logical_or(kk > thr_t, jnp.logical_and(eq, before < need))
            mask_ref[:, pl.ds(k0, tk)] = jnp.where(keep, 1, 0).astype(jnp.int8)
            return carry + jnp.sum(eqf, axis=1, keepdims=True)
        lax.fori_loop(0, nc, body, jnp.zeros((tq, 1), F32))

    def zero_chunk(c, carry):
        k0 = pl.multiple_of(c * tk, tk)
        mask_ref[:, pl.ds(k0, tk)] = jnp.zeros((tq, tk), jnp.int8)
        return carry
    lax.fori_loop(nc, s_total // tk, zero_chunk, 0)


def _prompt_select(iq, ikt, iw, n_sel):
    s = iq.shape[0]
    tq = _tile(s, 256)
    tk = _tile(tq, 256)
    kern = functools.partial(_idx_kernel, tq=tq, tk=tk, n_sel=n_sel)
    return pl.pallas_call(
        kern,
        grid=(s // tq,),
        in_specs=[pl.BlockSpec((tq, N_IDX_HEADS * IDX_DIM), lambda i: (i, 0)),
                  pl.BlockSpec((IDX_DIM, s), lambda i: (0, 0)),
                  pl.BlockSpec((tq, N_IDX_HEADS), lambda i: (i, 0))],
        out_specs=pl.BlockSpec((tq, s), lambda i: (i, 0)),
        out_shape=jax.ShapeDtypeStruct((s, s), jnp.int8),
        scratch_shapes=[pltpu.VMEM((tq, s), I32),
                        pltpu.VMEM((N_IDX_HEADS, tq, LANES), F32)],
        compiler_params=_params("arbitrary"),
        name="prompt_select",
    )(iq, ikt, iw)


def _attn_kernel(ii_ref, jj_ref, q_ref, k_ref, v_ref, mask_ref, bt_ref, fb_ref, o_ref,
                 m_ref, l_ref, acc_ref, *, n_heads, blk):
    step = pl.program_id(0)
    i = ii_ref[step]
    j = jj_ref[step]
    nb = blk // LANES
    scale = HEAD_DIM ** -0.5

    @pl.when(j == 0)
    def _():
        m_ref[...] = jnp.full(m_ref.shape, NEG_BIG, F32)
        l_ref[...] = jnp.zeros(l_ref.shape, F32)
        acc_ref[...] = jnp.zeros(acc_ref.shape, F32)

    def bias_block(mode, h):
        far = jnp.full((LANES, LANES), fb_ref[h], F32)
        rows = []
        for a in range(nb):
            cols = []
            for b in range(nb):
                dd = a - b if mode == "diag" else nb + a - b
                cols.append(bt_ref[0, h] if dd == 0 else bt_ref[1, h] if dd == 1 else far)
            rows.append(jnp.concatenate(cols, axis=1) if nb > 1 else cols[0])
        return jnp.concatenate(rows, axis=0) if nb > 1 else rows[0]

    def run(mode):
        valid = mask_ref[...].astype(F32) > 0.0
        for h in range(n_heads):
            sl = slice(h * HEAD_DIM, (h + 1) * HEAD_DIM)
            s = lax.dot_general(q_ref[:, sl], k_ref[:, sl], (((1,), (1,)), ((), ())),
                                preferred_element_type=F32) * scale
            if mode == "far":
                s = s + fb_ref[h]
            else:
                s = s + bias_block(mode, h)
            s = jnp.where(valid, s, -jnp.inf)
            m_old = m_ref[h]
            m_new = jnp.maximum(m_old, jnp.max(s, axis=1, keepdims=True))
            alpha = jnp.exp(m_old - m_new)
            p = jnp.exp(s - m_new)
            l_ref[h] = alpha * l_ref[h] + jnp.sum(p, axis=1, keepdims=True)
            acc_ref[h] = alpha * acc_ref[h] + jnp.dot(p.astype(BF16), v_ref[:, sl],
                                                      preferred_element_type=F32)
            m_ref[h] = m_new

    @pl.when(i == j)
    def _():
        run("diag")

    @pl.when(i == j + 1)
    def _():
        run("sub")

    @pl.when(i > j + 1)
    def _():
        run("far")

    @pl.when(i == j)
    def _():
        for h in range(n_heads):
            o_ref[:, h * HEAD_DIM:(h + 1) * HEAD_DIM] = (acc_ref[h] / l_ref[h]).astype(o_ref.dtype)


def _prompt_attention(q, k, v, mask, bias_tiles, far_bias):
    s, width = q.shape
    n_heads = width // HEAD_DIM
    blk = _tile(s, 512)
    nblk = s // blk
    ii = np.concatenate([np.full(i + 1, i, np.int32) for i in range(nblk)])
    jj = np.concatenate([np.arange(i + 1, dtype=np.int32) for i in range(nblk)])
    kern = functools.partial(_attn_kernel, n_heads=n_heads, blk=blk)
    grid_spec = pltpu.PrefetchScalarGridSpec(
        num_scalar_prefetch=2,
        grid=(len(ii),),
        in_specs=[pl.BlockSpec((blk, width), lambda t, ii, jj: (ii[t], 0)),
                  pl.BlockSpec((blk, width), lambda t, ii, jj: (jj[t], 0)),
                  pl.BlockSpec((blk, width), lambda t, ii, jj: (jj[t], 0)),
                  pl.BlockSpec((blk, blk), lambda t, ii, jj: (ii[t], jj[t])),
                  pl.BlockSpec((2, n_heads, LANES, LANES), lambda t, ii, jj: (0, 0, 0, 0)),
                  pl.BlockSpec(memory_space=pltpu.SMEM)],
        out_specs=pl.BlockSpec((blk, width), lambda t, ii, jj: (ii[t], 0)),
        scratch_shapes=[pltpu.VMEM((n_heads, blk, 1), F32),
                        pltpu.VMEM((n_heads, blk, 1), F32),
                        pltpu.VMEM((n_heads, blk, HEAD_DIM), F32)],
    )
    return pl.pallas_call(
        kern,
        grid_spec=grid_spec,
        out_shape=jax.ShapeDtypeStruct((s, width), BF16),
        compiler_params=_params("arbitrary"),
        name="prompt_attention",
    )(jnp.asarray(ii), jnp.asarray(jj), q, k, v, mask, bias_tiles, far_bias)


def _sidx_kernel(pt_ref, iq_ref, iw_ref, page_ref, o_ref):
    p = pl.program_id(1)
    kp = page_ref[0, 0].astype(BF16)
    d = lax.dot_general(iq_ref[0], kp, (((1,), (1,)), ((), ())), preferred_element_type=F32)
    sc = jnp.sum(jnp.maximum(d, 0.0) * iw_ref[0], axis=0, keepdims=True) + 0.0
    o_ref[0, pl.ds(p, 1), :] = sc


def _sample_scores(iq, iw, cache_idx_k, page_table):
    nb, n_pages = page_table.shape
    page = cache_idx_k.shape[2]
    grid_spec = pltpu.PrefetchScalarGridSpec(
        num_scalar_prefetch=1,
        grid=(nb, n_pages),
        in_specs=[pl.BlockSpec((1, N_IDX_HEADS, IDX_DIM), lambda b, p, pt: (b, 0, 0)),
                  pl.BlockSpec((1, N_IDX_HEADS, 1), lambda b, p, pt: (b, 0, 0)),
                  pl.BlockSpec((1, 1, page, IDX_DIM), lambda b, p, pt: (0, pt[b, p], 0, 0))],
        out_specs=pl.BlockSpec((1, n_pages, page), lambda b, p, pt: (b, 0, 0)),
    )
    return pl.pallas_call(
        _sidx_kernel,
        grid_spec=grid_spec,
        out_shape=jax.ShapeDtypeStruct((nb, n_pages, page), F32),
        compiler_params=_params("arbitrary", "arbitrary"),
        name="sample_scores",
    )(page_table, iq, iw, cache_idx_k)


def _sthr_kernel(sc_ref, iq_ref, iw_ref, ikn_ref, sel_ref, seln_ref, key_ref, *, n_sel, tk):
    nb, past = sc_ref.shape
    nc = past // tk
    nlane = tk // LANES

    prod = iq_ref[...].astype(F32) * ikn_ref[...].astype(F32)
    dn = jnp.sum(prod, axis=2)
    sn = jnp.sum(jnp.maximum(dn, 0.0) * iw_ref[...], axis=1, keepdims=True) + 0.0
    kn = _sort_key(jnp.broadcast_to(sn, (nb, LANES)))

    def to_keys(c, carry):
        k0 = pl.multiple_of(c * tk, tk)
        key_ref[:, pl.ds(k0, tk)] = _sort_key(sc_ref[:, pl.ds(k0, tk)])
        return carry
    lax.fori_loop(0, nc, to_keys, 0)

    def count_ge(cand):
        def body(c, cnt):
            k0 = pl.multiple_of(c * tk, tk)
            kk = key_ref[:, pl.ds(k0, tk)]
            for j in range(nlane):
                cnt = cnt + jnp.where(kk[:, j * LANES:(j + 1) * LANES] >= cand, 1, 0)
            return cnt
        cnt = lax.fori_loop(0, nc, body, jnp.zeros((nb, LANES), I32))
        return (jnp.sum(cnt.astype(F32), axis=1, keepdims=True)
                + jnp.where(kn[:, :1] >= cand[:, :1], 1.0, 0.0))

    thr = _bisect_threshold(count_ge, nb, n_sel)
    cgt = count_ge(thr + 1)
    need = n_sel - cgt
    upper = (lax.broadcasted_iota(I32, (tk, tk), 0)
             < lax.broadcasted_iota(I32, (tk, tk), 1)).astype(BF16)

    def body(c, carry):
        k0 = pl.multiple_of(c * tk, tk)
        kk = key_ref[:, pl.ds(k0, tk)]
        thr_t = jnp.concatenate([thr] * nlane, axis=1)
        eq = kk == thr_t
        eqf = jnp.where(eq, 1.0, 0.0)
        before = jnp.dot(eqf.astype(BF16), upper, preferred_element_type=F32) + carry
        keep = jnp.logical_or(kk > thr_t, jnp.logical_and(eq, before < need))
        sel_ref[:, pl.ds(k0, tk)] = jnp.where(keep, 1.0, 0.0)
        return carry + jnp.sum(eqf, axis=1, keepdims=True)
    n_eq = lax.fori_loop(0, nc, body, jnp.zeros((nb, 1), F32))
    keep_new = jnp.logical_or(kn > thr, jnp.logical_and(kn == thr, n_eq < need))
    seln_ref[...] = jnp.where(keep_new, 1.0, 0.0)


def _sample_select(scores, iq, iw, ik_new, n_sel):
    nb, past = scores.shape
    tk = _tile(past, 256)
    kern = functools.partial(_sthr_kernel, n_sel=n_sel, tk=tk)
    return pl.pallas_call(
        kern,
        out_shape=[jax.ShapeDtypeStruct((nb, past), F32),
                   jax.ShapeDtypeStruct((nb, LANES), F32)],
        scratch_shapes=[pltpu.VMEM((nb, past), I32)],
        compiler_params=_params(),
        name="sample_select",
    )(scores, iq, iw, ik_new)


def _sattn_kernel(pt_ref, q_ref, k_ref, v_ref, sel_ref, kn_ref, vn_ref, seln_ref, bt_ref, fb_ref,
                  o_ref, m_ref, l_ref, acc_ref, *, n_heads, rows):
    p = pl.program_id(1)
    n_pages = pl.num_programs(1)
    scale = HEAD_DIM ** -0.5

    @pl.when(p == 0)
    def _():
        m_ref[...] = jnp.full(m_ref.shape, NEG_BIG, F32)
        l_ref[...] = jnp.zeros(l_ref.shape, F32)
        acc_ref[...] = jnp.zeros(acc_ref.shape, F32)

    valid = sel_ref[0, pl.ds(p, 1), :] > 0.0
    last = p == n_pages - 1
    for h in range(n_heads):
        sl = slice(h * HEAD_DIM, (h + 1) * HEAD_DIM)
        qh = jnp.broadcast_to(q_ref[0, :, sl], (rows, HEAD_DIM))
        kh = k_ref[0, 0, :, sl].astype(BF16)
        vh = v_ref[0, 0, :, sl].astype(BF16)
        s = lax.dot_general(qh, kh, (((1,), (1,)), ((), ())), preferred_element_type=F32) * scale
        bias = jnp.where(last, bt_ref[1, h, 0:1, :], fb_ref[h])
        s = jnp.where(valid, s + bias, -jnp.inf)
        m_old = m_ref[h]
        m_new = jnp.maximum(m_old, jnp.max(s, axis=1, keepdims=True))
        alpha = jnp.exp(m_old - m_new)
        pr = jnp.exp(s - m_new)
        l_ref[h] = alpha * l_ref[h] + jnp.sum(pr, axis=1, keepdims=True)
        acc_ref[h] = alpha * acc_ref[h] + jnp.dot(pr.astype(BF16), vh, preferred_element_type=F32)
        m_ref[h] = m_new

    @pl.when(last)
    def _():
        vnew = seln_ref[0, :, 0:1] > 0.0
        for h in range(n_heads):
            sl = slice(h * HEAD_DIM, (h + 1) * HEAD_DIM)
            qh = q_ref[0, :, sl].astype(F32)
            kh = kn_ref[0, :, sl].astype(BF16).astype(F32)
            vh = vn_ref[0, :, sl].astype(BF16).astype(F32)
            s = jnp.sum(qh * kh, axis=1, keepdims=True) * scale + bt_ref[0, h, 0:1, 0:1]
            s = jnp.where(vnew, s, -jnp.inf)
            m_old = m_ref[h]
            m_new = jnp.maximum(m_old, s)
            alpha = jnp.exp(m_old - m_new)
            pr = jnp.exp(s - m_new)
            l_new = alpha * l_ref[h] + pr
            acc = alpha * acc_ref[h] + pr.astype(BF16).astype(F32) * vh
            o_ref[0, :, sl] = (acc[0:1] / l_new[0:1]).astype(o_ref.dtype)


def _sample_attention(q, k_new, v_new, cache_k, cache_v, sel, sel_new, page_table, bias_tiles, far_bias):
    nb, n_pages = page_table.shape
    _, n_phys, page, width = cache_k.shape
    n_heads = width // HEAD_DIM
    rows = 16
    kern = functools.partial(_sattn_kernel, n_heads=n_heads, rows=rows)
    per_b = lambda b, p, pt: (b, 0, 0)
    grid_spec = pltpu.PrefetchScalarGridSpec(
        num_scalar_prefetch=1,
        grid=(nb, n_pages),
        in_specs=[pl.BlockSpec((1, 1, width), per_b),
                  pl.BlockSpec((1, 1, page, width), lambda b, p, pt: (0, pt[b, p], 0, 0)),
                  pl.BlockSpec((1, 1, page, width), lambda b, p, pt: (0, pt[b, p], 0, 0)),
                  pl.BlockSpec((1, n_pages, page), per_b),
                  pl.BlockSpec((1, 1, width), per_b),
                  pl.BlockSpec((1, 1, width), per_b),
                  pl.BlockSpec((1, 1, LANES), per_b),
                  pl.BlockSpec((2, n_heads, LANES, LANES), lambda b, p, pt: (0, 0, 0, 0)),
                  pl.BlockSpec(memory_space=pltpu.SMEM)],
        out_specs=pl.BlockSpec((1, 1, width), per_b),
        scratch_shapes=[pltpu.VMEM((n_heads, rows, 1), F32),
                        pltpu.VMEM((n_heads, rows, 1), F32),
                        pltpu.VMEM((n_heads, rows, HEAD_DIM), F32)],
    )
    return pl.pallas_call(
        kern,
        grid_spec=grid_spec,
        out_shape=jax.ShapeDtypeStruct((nb, 1, width), BF16),
        compiler_params=_params("arbitrary", "arbitrary"),
        name="sample_attention",
    )(page_table, q, cache_k, cache_v, sel, k_new, v_new, sel_new, bias_tiles, far_bias)


def _pool_kernel(u_ref, prev_ref, w_ref, ps_ref, o_ref, *, tm, pos0):
    i = pl.program_id(1)
    cur = u_ref[0]
    halo = POOL_CTX + 1
    prev = jnp.where(i == 0, 0.0, prev_ref[0])
    ext = jnp.concatenate([prev, cur], axis=0)
    gdim = cur.shape[1] // len(POOL_WINDOWS)
    t = pos0 + i * tm + lax.broadcasted_iota(I32, (tm, 1), 0)
    for g, win in enumerate(POOL_WINDOWS):
        cs = slice(g * gdim, (g + 1) * gdim)
        s = ext[:, cs]
        span = 1
        while span < win:
            s = s[span:] + s[:-span]
            span *= 2
        wsum = s[s.shape[0] - tm:]
        cnt = jnp.minimum(t + 1, win).astype(F32)
        pooled = (wsum / cnt - cur[:, cs]).astype(BF16)
        mixed = jnp.dot(pooled, w_ref[g], preferred_element_type=F32)
        o_ref[0, :, cs] = (mixed * ps_ref[:, cs]).astype(o_ref.dtype)


def _pool(u, w_pool, pool_scale, pos0):
    nb, n, width = u.shape
    halo = POOL_CTX + 1
    tm = _tile(n, 512)
    ratio = tm // halo
    kern = functools.partial(_pool_kernel, tm=tm, pos0=pos0)
    return pl.pallas_call(
        kern,
        grid=(nb, n // tm),
        in_specs=[pl.BlockSpec((1, tm, width), lambda b, i: (b, i, 0)),
                  pl.BlockSpec((1, halo, width), lambda b, i: (b, jnp.maximum(i * ratio - 1, 0), 0)),
                  pl.BlockSpec(w_pool.shape, lambda b, i: (0, 0, 0)),
                  pl.BlockSpec((1, width), lambda b, i: (0, 0))],
        out_specs=pl.BlockSpec((1, tm, width), lambda b, i: (b, i, 0)),
        out_shape=jax.ShapeDtypeStruct((nb, n, width), BF16),
        compiler_params=_params("arbitrary", "arbitrary"),
        name="pool",
    )(u, u, w_pool, pool_scale)


def _gateproj_kernel(a_ref, p_ref, wa_ref, wp_ref, ga_ref, gb_ref, o_ref):
    a = jnp.dot(a_ref[...], wa_ref[...], preferred_element_type=F32)
    p = jnp.dot(p_ref[...], wp_ref[...], preferred_element_type=F32)
    o_ref[...] = (_sigmoid(ga_ref[...]) * a + _sigmoid(gb_ref[...]) * p).astype(o_ref.dtype)


def _gateproj(attn, pooled, wa, wp, ga, gb):
    m, ka = attn.shape
    kp = pooled.shape[1]
    n = wa.shape[1]
    tm = _tile(m, 512)
    tn = _tile(n, 512)
    return pl.pallas_call(
        _gateproj_kernel,
        grid=(m // tm, n // tn),
        in_specs=[pl.BlockSpec((tm, ka), lambda i, j: (i, 0)),
                  pl.BlockSpec((tm, kp), lambda i, j: (i, 0)),
                  pl.BlockSpec((ka, tn), lambda i, j: (0, j)),
                  pl.BlockSpec((kp, tn), lambda i, j: (0, j)),
                  pl.BlockSpec((tm, tn), lambda i, j: (i, j)),
                  pl.BlockSpec((tm, tn), lambda i, j: (i, j))],
        out_specs=pl.BlockSpec((tm, tn), lambda i, j: (i, j)),
        out_shape=jax.ShapeDtypeStruct((m, n), BF16),
        compiler_params=_params("arbitrary", "arbitrary"),
        name="gateproj",
    )(attn, pooled, wa, wp, ga, gb)


def _mixout_kernel(m_ref, w_ref, x_ref, gate_ref, gpost_ref, gpre_ref, sc_ref, sh_ref, x1_ref, h2_ref):
    y = jnp.dot(m_ref[...], w_ref[...], preferred_element_type=F32)
    x1 = x_ref[...] + gate_ref[...] * _rms(y, gpost_ref[...])
    x1_ref[...] = x1
    h2_ref[...] = (_rms(x1, gpre_ref[...]) * (1 + sc_ref[...]) + sh_ref[...]).astype(h2_ref.dtype)


def _mixout(m, w_out, x, gate1, g_post, g_pre_ffn, scale2, shift2):
    t, d = x.shape
    tm = _tile(t, 256)
    row = pl.BlockSpec((1, d), lambda i: (0, 0))
    tok = pl.BlockSpec((tm, d), lambda i: (i, 0))
    return pl.pallas_call(
        _mixout_kernel,
        grid=(t // tm,),
        in_specs=[tok, pl.BlockSpec((d, d), lambda i: (0, 0)), tok, _mod_spec(gate1, tm, d),
                  row, row, _mod_spec(scale2, tm, d), _mod_spec(shift2, tm, d)],
        out_specs=[tok, tok],
        out_shape=[jax.ShapeDtypeStruct((t, d), F32), jax.ShapeDtypeStruct((t, d), BF16)],
        compiler_params=_params("arbitrary"),
        name="mixout",
    )(m, w_out, x, gate1, g_post, g_pre_ffn, scale2, shift2)


def _router_kernel(h_ref, wt_ref, b_ref, o_ref):
    n_exp, tm = o_ref.shape
    per_group = n_exp // N_GROUP
    logits = lax.dot_general(wt_ref[...], h_ref[...], (((1,), (1,)), ((), ())),
                             preferred_element_type=F32)
    s = _sigmoid(logits)
    sb = s + b_ref[...]
    gidx = lax.broadcasted_iota(I32, (per_group, tm), 0)
    gscores = []
    for g in range(N_GROUP):
        x = sb[g * per_group:(g + 1) * per_group]
        m1 = jnp.max(x, axis=0, keepdims=True)
        i1 = jnp.min(jnp.where(x == m1, gidx, per_group), axis=0, keepdims=True)
        m2 = jnp.max(jnp.where(gidx == i1, -jnp.inf, x), axis=0, keepdims=True)
        gscores.append(m1 + m2)
    cur = jnp.concatenate(gscores, axis=0)
    ridx = lax.broadcasted_iota(I32, (N_GROUP, tm), 0)
    gsel = jnp.zeros((N_GROUP, tm), F32)
    for _ in range(TOPK_GROUP):
        m = jnp.max(cur, axis=0, keepdims=True)
        im = jnp.min(jnp.where(cur == m, ridx, N_GROUP), axis=0, keepdims=True)
        pick = ridx == im
        gsel = jnp.where(pick, 1.0, gsel)
        cur = jnp.where(pick, -jnp.inf, cur)
    emask = jnp.concatenate(
        [jnp.broadcast_to(gsel[g:g + 1], (per_group, tm)) for g in range(N_GROUP)], axis=0)
    cur = jnp.where(emask > 0.0, sb, -jnp.inf)
    eidx = lax.broadcasted_iota(I32, (n_exp, tm), 0)
    esel = jnp.zeros((n_exp, tm), F32)
    for _ in range(TOP_K):
        m = jnp.max(cur, axis=0, keepdims=True)
        im = jnp.min(jnp.where(cur == m, eidx, n_exp), axis=0, keepdims=True)
        pick = eidx == im
        esel = jnp.where(pick, 1.0, esel)
        cur = jnp.where(pick, -jnp.inf, cur)
    w = jnp.where(esel > 0.0, s, 0.0)
    o_ref[...] = w / jnp.sum(w, axis=0, keepdims=True) * ROUTED_SCALE


def _router(h, w_router_t, b_router):
    t, d = h.shape
    n_exp = w_router_t.shape[0]
    tm = _tile(t, 512)
    return pl.pallas_call(
        _router_kernel,
        grid=(t // tm,),
        in_specs=[pl.BlockSpec((tm, d), lambda i: (i, 0)),
                  pl.BlockSpec((n_exp, d), lambda i: (0, 0)),
                  pl.BlockSpec((n_exp, 1), lambda i: (0, 0))],
        out_specs=pl.BlockSpec((n_exp, tm), lambda i: (0, i)),
        out_shape=jax.ShapeDtypeStruct((n_exp, t), F32),
        compiler_params=_params("arbitrary"),
        name="router",
    )(h, w_router_t, b_router)


def _moe_kernel(h_ref, g_ref, wg_ref, wu_ref, wd_ref, init_ref, x_ref, gate_ref, gpost_ref,
                o_ref, acc_ref, *, final):
    e = pl.program_id(1)
    n_exp = pl.num_programs(1)

    @pl.when(e == 0)
    def _():
        acc_ref[...] = init_ref[...]

    gates = g_ref[...]
    lane = lax.broadcasted_iota(I32, gates.shape, 1)
    gcol = jnp.sum(jnp.where(lane == e, gates, 0.0), axis=1, keepdims=True)

    @pl.when(jnp.max(jnp.abs(gcol)) > 0.0)
    def _():
        hh = h_ref[...]
        a = jnp.dot(hh, wg_ref[0], preferred_element_type=F32)
        b = jnp.dot(hh, wu_ref[0], preferred_element_type=F32)
        act = (a * _sigmoid(a) * b).astype(BF16)
        y = jnp.dot(act, wd_ref[0], preferred_element_type=F32)
        acc_ref[...] += y * gcol

    @pl.when(e == n_exp - 1)
    def _():
        if final:
            o_ref[...] = x_ref[...] + gate_ref[...] * _rms(acc_ref[...], gpost_ref[...])
        else:
            o_ref[...] = acc_ref[...]


def _moe(h, gates, wg, wu, wd, init, x, gate2, g_post, final):
    t, d = h.shape
    n_exp, _, f = wg.shape
    tm = _tile(t, 512)
    tok = pl.BlockSpec((tm, d), lambda i, e: (i, 0))
    kern = functools.partial(_moe_kernel, final=final)
    return pl.pallas_call(
        kern,
        grid=(t // tm, n_exp),
        in_specs=[tok,
                  pl.BlockSpec((tm, gates.shape[1]), lambda i, e: (i, 0)),
                  pl.BlockSpec((1, d, f), lambda i, e: (e, 0, 0)),
                  pl.BlockSpec((1, d, f), lambda i, e: (e, 0, 0)),
                  pl.BlockSpec((1, f, d), lambda i, e: (e, 0, 0)),
                  tok, tok, _mod_spec(gate2, tm, d),
                  pl.BlockSpec((1, d), lambda i, e: (0, 0))],
        out_specs=tok,
        out_shape=jax.ShapeDtypeStruct((t, d), F32),
        scratch_shapes=[pltpu.VMEM((tm, d), F32)],
        compiler_params=_params("arbitrary", "arbitrary"),
        name="moe_final" if final else "moe_shared",
    )(h, gates, wg, wu, wd, init, x, gate2, g_post)


def _token_path(x, mods, wts):
    sh1, sc1 = mods[0], mods[1]
    h = _normmod(x, wts["g_pre_mix"], sc1, sh1)
    q = _mm(h, wts["w_q"], BF16)
    k = _mm(h, wts["w_k"], F32)
    v = _mm(h, wts["w_v"], F32)
    iq = _mm(h, wts["w_iq"], BF16)
    u = _mm(h, wts["w_u"], F32)
    ga = _mm(h, wts["w_ga"], F32)
    gb = _mm(h, wts["w_gb"], F32)
    ik, iw = _idxproj(h, wts["w_idx"], wts["idx_k_norm"])
    return q, k, v, iq, iw, ik, u, ga, gb


def _back_half(x, attn, pooled, ga, gb, mods, wts):
    _, _, g1, sh2, sc2, g2 = mods
    m = _gateproj(attn, pooled, wts["w_proj_attn"], wts["w_proj_pool"], ga, gb)
    x1, h2 = _mixout(m, wts["w_out"], x, g1, wts["g_post_mix"], wts["g_pre_ffn"], sc2, sh2)
    gates = _router(h2, wts["w_router_t"], wts["b_router"]).T
    t, d = x.shape
    ones = jnp.ones((t, 1), F32)
    zeros = jnp.zeros((t, d), F32)
    shared = _moe(h2, ones, wts["w_gate_s"], wts["w_up_s"], wts["w_down_s"], zeros, x1, g2,
                  wts["g_post_ffn"], final=False)
    return _moe(h2, gates, wts["w_gate_e"], wts["w_up_e"], wts["w_down_e"], shared, x1, g2,
                wts["g_post_ffn"], final=True)


def kernel(x_prompt, x_sample, c_prompt, c_sample, cache_k, cache_v, cache_idx_k, state_pool, page_table, w_ada, b_ada, g_pre_mix, w_in, idx_k_norm, rel_bias, w_pool, pool_scale, w_proj_attn, w_proj_pool, w_out, g_post_mix, g_pre_ffn, w_router, b_router, w_gate_e, w_up_e, w_down_e, w_gate_s, w_up_s, w_down_s, g_post_ffn):
    depth = w_ada.shape[0]
    assert depth == 1, "single-layer trunk"
    nbp, seq, d = x_prompt.shape
    assert nbp == 1, "one prompt sequence"
    nbs, dec_seq, _ = x_sample.shape
    assert dec_seq == 1, "one new token per sample sequence"
    _, n_phys, page, n_heads, head_dim = cache_k.shape
    assert head_dim == HEAD_DIM and page == LANES
    aw = n_heads * HEAD_DIM
    pw = state_pool.shape[-1]
    n_pages = page_table.shape[1]
    past = n_pages * page
    iqw = N_IDX_HEADS * IDX_DIM

    win = w_in[0]
    offs = np.cumsum([0, aw, aw, aw, iqw, IDX_DIM, N_IDX_HEADS, pw, d, d])
    seg = lambda a: win[:, offs[a]:offs[a + 1]].astype(BF16)
    w_idx = jnp.pad(win[:, offs[4]:offs[6]], ((0, 0), (0, LANES - IDX_DIM - N_IDX_HEADS))).astype(BF16)
    wts = dict(
        g_pre_mix=g_pre_mix, idx_k_norm=idx_k_norm,
        w_q=seg(0), w_k=seg(1), w_v=seg(2), w_iq=seg(3), w_idx=w_idx, w_u=seg(6), w_ga=seg(7), w_gb=seg(8),
        w_proj_attn=w_proj_attn[0].astype(BF16), w_proj_pool=w_proj_pool[0].astype(BF16),
        w_out=w_out[0].astype(BF16), g_post_mix=g_post_mix, g_pre_ffn=g_pre_ffn,
        w_router_t=w_router[0].T.astype(BF16), b_router=b_router.reshape(-1, 1),
        w_gate_e=w_gate_e[0].astype(BF16), w_up_e=w_up_e[0].astype(BF16), w_down_e=w_down_e[0].astype(BF16),
        w_gate_s=w_gate_s.astype(BF16), w_up_s=w_up_s.astype(BF16), w_down_s=w_down_s.astype(BF16),
        g_post_ffn=g_post_ffn,
    )
    w_pool_b = w_pool[0].astype(BF16)

    n_c = nbp + nbs
    rows = -(-n_c // SUBLANES) * SUBLANES
    c_all = jnp.pad(jnp.concatenate([c_prompt, c_sample], axis=0), ((0, rows - n_c), (0, 0)))
    mod = _ada(c_all, w_ada[0], b_ada)
    mods_p = [mod[0:1, a * d:(a + 1) * d] for a in range(6)]
    mods_s = [mod[nbp:n_c, a * d:(a + 1) * d] for a in range(6)]

    bias_tiles = _bias_tiles(rel_bias)
    far_bias = rel_bias[N_BUCKETS - 1]

    xp = x_prompt.reshape(seq, d)
    q, k, v, iq, iw, ik, u, ga, gb = _token_path(xp, mods_p, wts)
    mask = _prompt_select(iq, ik.astype(BF16).T, iw, min(TOPK_MAX, seq // 4))
    attn = _prompt_attention(q, k.astype(BF16), v.astype(BF16), mask, bias_tiles, far_bias)
    pooled = _pool(u.reshape(1, seq, pw), w_pool_b, pool_scale, 0).reshape(seq, pw)
    y_prompt = _back_half(xp, attn, pooled, ga, gb, mods_p, wts).reshape(nbp, seq, d)
    k_prompt = k.reshape(1, nbp, seq, n_heads, HEAD_DIM)
    v_prompt = v.reshape(1, nbp, seq, n_heads, HEAD_DIM)
    idxk_prompt = ik.reshape(1, nbp, seq, IDX_DIM)
    pool_prompt = u[seq - POOL_CTX:].reshape(1, nbp, POOL_CTX, pw)

    xs = x_sample.reshape(nbs, d)
    qs, ks, vs, iqs, iws, iks, us, gas, gbs = _token_path(xs, mods_s, wts)
    iq3 = iqs.reshape(nbs, N_IDX_HEADS, IDX_DIM)
    iw3 = iws.reshape(nbs, N_IDX_HEADS, 1)
    scores = _sample_scores(iq3, iw3, cache_idx_k, page_table).reshape(nbs, past)
    sel, sel_new = _sample_select(scores, iq3, iws, iks.astype(BF16).reshape(nbs, 1, IDX_DIM),
                                  min(TOPK_MAX, (past + dec_seq) // 4))
    attn_s = _sample_attention(
        qs.reshape(nbs, 1, aw), ks.reshape(nbs, 1, aw), vs.reshape(nbs, 1, aw),
        cache_k.reshape(1, n_phys, page, aw), cache_v.reshape(1, n_phys, page, aw),
        sel.reshape(nbs, n_pages, page), sel_new.reshape(nbs, 1, LANES), page_table,
        bias_tiles, far_bias).reshape(nbs, aw)
    ext = jnp.concatenate([state_pool[0], us.reshape(nbs, 1, pw)], axis=1)
    pooled_s = _pool(ext, w_pool_b, pool_scale, past - POOL_CTX)[:, POOL_CTX]
    y_sample = _back_half(xs, attn_s, pooled_s, gas, gbs, mods_s, wts).reshape(nbs, 1, d)
    k_sample = ks.reshape(1, nbs, 1, n_heads, HEAD_DIM)
    v_sample = vs.reshape(1, nbs, 1, n_heads, HEAD_DIM)
    idxk_sample = iks.reshape(1, nbs, 1, IDX_DIM)
    pool_sample = ext[:, 1:].reshape(1, nbs, POOL_CTX, pw)

    return (y_prompt, y_sample, k_prompt, v_prompt, idxk_prompt, pool_prompt,
            k_sample, v_sample, idxk_sample, pool_sample)
```

```python
import functools
import math

import jax
import jax.numpy as jnp
import numpy as np
from jax import lax
from jax.experimental import pallas as pl
from jax.experimental.pallas import tpu as pltpu

F32 = jnp.float32
BF16 = jnp.bfloat16
I32 = jnp.int32

HEAD_DIM = 128
N_IDX_HEADS = 16
IDX_DIM = 64
TOPK_MAX = 256
N_BUCKETS = 32
MAX_DISTANCE = 128
POOL_WINDOWS = (2, 4, 8, 16)
POOL_CTX = max(POOL_WINDOWS) - 1
N_GROUP = 8
TOPK_GROUP = 4
TOP_K = 8
ROUTED_SCALE = 2.5
EPS = 1e-6

LANES = 128
SUBLANES = 8
VMEM_LIMIT_BYTES = 56 * 1024 * 1024

NEG_BIG = -1e30
NEG_INF_KEY = int(np.int32(np.uint32(0xFF800000)) ^ np.int32(0x7FFFFFFF))
INT_MIN = -(2 ** 31)


def _params(*sem):
    return pltpu.CompilerParams(dimension_semantics=sem, vmem_limit_bytes=VMEM_LIMIT_BYTES)


def _tile(n, pref):
    if n <= pref:
        return n
    t = pref
    while n % t:
        t //= 2
    return t


def _rms(x, g):
    return x * lax.rsqrt(jnp.mean(x * x, axis=-1, keepdims=True) + EPS) * g


def _sigmoid(x):
    return jax.nn.sigmoid(x)


def _ada_kernel(c_ref, w_ref, b_ref, o_ref):
    c = c_ref[...]
    a = (c * _sigmoid(c)).astype(BF16)
    o_ref[...] = jnp.dot(a, w_ref[...].astype(BF16), preferred_element_type=F32) + b_ref[...]


def _ada(c, w, b):
    r, d = c.shape
    n = w.shape[1]
    tn = _tile(n, 1024)
    return pl.pallas_call(
        _ada_kernel,
        grid=(n // tn,),
        in_specs=[pl.BlockSpec((r, d), lambda j: (0, 0)),
                  pl.BlockSpec((d, tn), lambda j: (0, j)),
                  pl.BlockSpec((1, tn), lambda j: (0, j))],
        out_specs=pl.BlockSpec((r, tn), lambda j: (0, j)),
        out_shape=jax.ShapeDtypeStruct((r, n), F32),
        compiler_params=_params("arbitrary"),
        name="ada",
    )(c, w, b)


def _normmod_kernel(x_ref, g_ref, sc_ref, sh_ref, o_ref):
    y = _rms(x_ref[...], g_ref[...])
    o_ref[...] = (y * (1 + sc_ref[...]) + sh_ref[...]).astype(o_ref.dtype)


def _mod_spec(mod, tm, d):
    if mod.shape[0] == 1:
        return pl.BlockSpec((1, d), lambda i, *_: (0, 0))
    return pl.BlockSpec((tm, d), lambda i, *_: (i, 0))


def _normmod(x, g, scale, shift):
    t, d = x.shape
    tm = _tile(t, 512)
    return pl.pallas_call(
        _normmod_kernel,
        grid=(t // tm,),
        in_specs=[pl.BlockSpec((tm, d), lambda i: (i, 0)),
                  pl.BlockSpec((1, d), lambda i: (0, 0)),
                  _mod_spec(scale, tm, d), _mod_spec(shift, tm, d)],
        out_specs=pl.BlockSpec((tm, d), lambda i: (i, 0)),
        out_shape=jax.ShapeDtypeStruct((t, d), BF16),
        compiler_params=_params("arbitrary"),
        name="normmod",
    )(x, g, scale, shift)


def _mm_kernel(a_ref, w_ref, o_ref):
    o_ref[...] = jnp.dot(a_ref[...], w_ref[...], preferred_element_type=F32).astype(o_ref.dtype)


def _mm(a, w, out_dtype):
    m, k = a.shape
    n = w.shape[1]
    tm = _tile(m, 512)
    tn = _tile(n, 512)
    return pl.pallas_call(
        _mm_kernel,
        grid=(m // tm, n // tn),
        in_specs=[pl.BlockSpec((tm, k), lambda i, j: (i, 0)),
                  pl.BlockSpec((k, tn), lambda i, j: (0, j))],
        out_specs=pl.BlockSpec((tm, tn), lambda i, j: (i, j)),
        out_shape=jax.ShapeDtypeStruct((m, n), out_dtype),
        compiler_params=_params("arbitrary", "arbitrary"),
        name="mm",
    )(a, w)


def _idxproj_kernel(a_ref, w_ref, g_ref, ik_ref, iw_ref):
    z = jnp.dot(a_ref[...], w_ref[...], preferred_element_type=F32)
    ik = z[:, :IDX_DIM]
    xc = ik - jnp.mean(ik, axis=-1, keepdims=True)
    y = xc * lax.rsqrt(jnp.mean(xc * xc, axis=-1, keepdims=True) + EPS)
    ik_ref[...] = y * g_ref[...]
    iw_ref[...] = z[:, IDX_DIM:IDX_DIM + N_IDX_HEADS] * (IDX_DIM ** -0.5 * N_IDX_HEADS ** -0.5)


def _idxproj(a, w, g):
    m, k = a.shape
    tm = _tile(m, 512)
    return pl.pallas_call(
        _idxproj_kernel,
        grid=(m // tm,),
        in_specs=[pl.BlockSpec((tm, k), lambda i: (i, 0)),
                  pl.BlockSpec((k, LANES), lambda i: (0, 0)),
                  pl.BlockSpec((1, IDX_DIM), lambda i: (0, 0))],
        out_specs=[pl.BlockSpec((tm, IDX_DIM), lambda i: (i, 0)),
                   pl.BlockSpec((tm, N_IDX_HEADS), lambda i: (i, 0))],
        out_shape=[jax.ShapeDtypeStruct((m, IDX_DIM), F32),
                   jax.ShapeDtypeStruct((m, N_IDX_HEADS), F32)],
        compiler_params=_params("arbitrary"),
        name="idxproj",
    )(a, w, g)


def _t5_bucket(rel):
    n = jnp.maximum(rel, 0)
    max_exact = N_BUCKETS // 2
    nf = jnp.maximum(n, 1).astype(F32)
    large = max_exact + (jnp.log(nf / max_exact) / math.log(MAX_DISTANCE / max_exact)
                         * (N_BUCKETS - max_exact)).astype(I32)
    large = jnp.minimum(large, N_BUCKETS - 1)
    return jnp.where(n < max_exact, n, large)


def _bias_kernel(rb_ref, o_ref):
    n_heads = o_ref.shape[1]
    r = lax.broadcasted_iota(I32, (LANES, LANES), 0)
    c = lax.broadcasted_iota(I32, (LANES, LANES), 1)
    for d in range(2):
        bucket = _t5_bucket(d * LANES + r - c)
        for h in range(n_heads):
            val = jnp.zeros((LANES, LANES), F32)
            for b in range(N_BUCKETS):
                val = jnp.where(bucket == b, rb_ref[b, h], val)
            o_ref[d, h] = val - rb_ref[N_BUCKETS - 1, h]


def _bias_tiles(rel_bias):
    n_heads = rel_bias.shape[1]
    return pl.pallas_call(
        _bias_kernel,
        in_specs=[pl.BlockSpec(memory_space=pltpu.SMEM)],
        out_shape=jax.ShapeDtypeStruct((2, n_heads, LANES, LANES), F32),
        name="bias_tiles",
    )(rel_bias)


def _sort_key(x):
    bits = pltpu.bitcast(x, I32)
    return jnp.where(bits < 0, bits ^ 0x7FFFFFFF, bits)


def _bisect_threshold(count_ge, rows, n_sel):
    def body(it, prefix):
        bit = 31 - it
        cand = prefix + jnp.left_shift(jnp.int32(1), bit)
        cnt = count_ge(cand)
        return jnp.where(cnt >= n_sel, cand, prefix)

    return lax.fori_loop(0, 32, body, jnp.full((rows, LANES), INT_MIN, I32))


def _idx_kernel(iq_ref, ikt_ref, iw_ref, mask_ref, key_ref, wb_ref, *, tq, tk, ck, rg, n_sel):
    i = pl.program_id(0)
    s_total = ikt_ref.shape[1]
    q0 = i * tq
    nc = (q0 + tq) // tk
    nlane = tk // LANES

    for h in range(N_IDX_HEADS):
        wb_ref[h] = jnp.broadcast_to(iw_ref[:, h:h + 1], (tq, LANES))

    def score_chunk(c, carry):
        k0 = pl.multiple_of(c * tk, tk)
        kt = ikt_ref[:, pl.ds(k0, tk)]
        acc = jnp.zeros((tq, tk), F32)
        for h in range(N_IDX_HEADS):
            d = jnp.dot(iq_ref[:, h * IDX_DIM:(h + 1) * IDX_DIM], kt, preferred_element_type=F32)
            w = jnp.concatenate([wb_ref[h]] * nlane, axis=1)
            acc = acc + jnp.maximum(d, 0.0) * w
        acc = acc + 0.0
        t = q0 + lax.broadcasted_iota(I32, (tq, tk), 0)
        s = k0 + lax.broadcasted_iota(I32, (tq, tk), 1)
        acc = jnp.where(s <= t, acc, -jnp.inf)
        key_ref[:, pl.ds(k0, tk)] = _sort_key(acc)
        return carry

    lax.fori_loop(0, nc, score_chunk, 0)

    ncc = (nc * tk + ck - 1) // ck

    def fill_chunk(c, carry):
        key_ref[:, pl.ds(pl.multiple_of(c * tk, tk), tk)] = jnp.full((tq, tk), NEG_INF_KEY, I32)
        return carry
    lax.fori_loop(nc, ncc * (ck // tk), fill_chunk, 0)

    ones = jnp.ones((LANES, LANES), BF16)

    def count_ge(cand):
        parts = []
        for g in range(tq // rg):
            rows = pl.ds(g * rg, rg)
            cand_g = cand[g * rg:(g + 1) * rg]

            def body(c, cnt, rows=rows, cand_g=cand_g):
                kk = key_ref[rows, pl.ds(pl.multiple_of(c * ck, ck), ck)]
                for j in range(ck // LANES):
                    cnt = cnt + jnp.where(kk[:, j * LANES:(j + 1) * LANES] >= cand_g, 1, 0)
                return cnt
            parts.append(lax.fori_loop(0, ncc, body, jnp.zeros((rg, LANES), I32)))
        cnt = jnp.concatenate(parts, axis=0) if len(parts) > 1 else parts[0]
        return jnp.dot(cnt.astype(F32).astype(BF16), ones, preferred_element_type=F32)

    thr = _bisect_threshold(count_ge, tq, n_sel)
    is_neg = thr == NEG_INF_KEY
    thr_eff = jnp.where(is_neg, NEG_INF_KEY + 1, thr)
    cge = count_ge(thr)
    tied = jnp.logical_and(jnp.logical_not(is_neg), cge > n_sel)
    any_tied = jnp.max(jnp.where(tied, 1.0, 0.0)) > 0.0

    @pl.when(jnp.logical_not(any_tied))
    def _():
        def body(c, carry):
            k0 = pl.multiple_of(c * tk, tk)
            kk = key_ref[:, pl.ds(k0, tk)]
            thr_t = jnp.concatenate([thr_eff] * nlane, axis=1)
            mask_ref[:, pl.ds(k0, tk)] = jnp.where(kk >= thr_t, 1, 0).astype(jnp.int8)
            return carry
        lax.fori_loop(0, nc, body, 0)

    @pl.when(any_tied)
    def _():
        cgt = count_ge(thr + 1)
        need = jnp.where(is_neg[:, :1], 0.0, n_sel - cgt[:, :1])
        upper = (lax.broadcasted_iota(I32, (tk, tk), 0)
                 < lax.broadcasted_iota(I32, (tk, tk), 1)).astype(BF16)

        def body(c, carry):
            k0 = pl.multiple_of(c * tk, tk)
            kk = key_ref[:, pl.ds(k0, tk)]
            thr_t = jnp.concatenate([thr] * nlane, axis=1)
            eq = kk == thr_t
            eqf = jnp.where(eq, 1.0, 0.0)
            before = jnp.dot(eqf.astype(BF16), upper, preferred_element_type=F32) + carry
            keep = jnp.logical_or(kk > thr_t, jnp.logical_and(eq, before < need))
            mask_ref[:, pl.ds(k0, tk)] = jnp.where(keep, 1, 0).astype(jnp.int8)
            return carry + jnp.sum(eqf, axis=1, keepdims=True)
        lax.fori_loop(0, nc, body, jnp.zeros((tq, 1), F32))

    def zero_chunk(c, carry):
        k0 = pl.multiple_of(c * tk, tk)
        mask_ref[:, pl.ds(k0, tk)] = jnp.zeros((tq, tk), jnp.int8)
        return carry
    lax.fori_loop(nc, s_total // tk, zero_chunk, 0)


def _prompt_select(iq, ikt, iw, n_sel):
    s = iq.shape[0]
    tq = _tile(s, 256)
    tk = _tile(tq, 256)
    ck = 2 * tk if s % (2 * tk) == 0 else tk
    assert s // LANES <= 256, "per-lane key counts must stay exact in bf16"
    kern = functools.partial(_idx_kernel, tq=tq, tk=tk, ck=ck, rg=_tile(tq, 64), n_sel=n_sel)
    return pl.pallas_call(
        kern,
        grid=(s // tq,),
        in_specs=[pl.BlockSpec((tq, N_IDX_HEADS * IDX_DIM), lambda i: (i, 0)),
                  pl.BlockSpec((IDX_DIM, s), lambda i: (0, 0)),
                  pl.BlockSpec((tq, N_IDX_HEADS), lambda i: (i, 0))],
        out_specs=pl.BlockSpec((tq, s), lambda i: (i, 0)),
        out_shape=jax.ShapeDtypeStruct((s, s), jnp.int8),
        scratch_shapes=[pltpu.VMEM((tq, s), I32),
                        pltpu.VMEM((N_IDX_HEADS, tq, LANES), F32)],
        compiler_params=_params("arbitrary"),
        name="prompt_select",
    )(iq, ikt, iw)


def _attn_kernel(ii_ref, jj_ref, q_ref, k_ref, v_ref, mask_ref, bt_ref, o_ref,
                 m_ref, l_ref, acc_ref, *, n_heads, blk):
    step = pl.program_id(0)
    i = ii_ref[step]
    j = jj_ref[step]
    nb = blk // LANES
    scale = HEAD_DIM ** -0.5

    @pl.when(j == 0)
    def _():
        m_ref[...] = jnp.full(m_ref.shape, NEG_BIG, F32)
        l_ref[...] = jnp.zeros(l_ref.shape, F32)
        acc_ref[...] = jnp.zeros(acc_ref.shape, F32)

    def bias_block(mode, h):
        zero = jnp.zeros((LANES, LANES), F32)
        rows = []
        for a in range(nb):
            cols = []
            for b in range(nb):
                dd = a - b if mode == "diag" else nb + a - b
                cols.append(bt_ref[0, h] if dd == 0 else bt_ref[1, h] if dd == 1 else zero)
            rows.append(jnp.concatenate(cols, axis=1) if nb > 1 else cols[0])
        return jnp.concatenate(rows, axis=0) if nb > 1 else rows[0]

    def run(mode):
        mbias = jnp.where(mask_ref[...].astype(F32) > 0.0, 0.0, -jnp.inf)
        m_all, l_all, acc_all = m_ref[...], l_ref[...], acc_ref[...]
        m_out, l_out, acc_out = [], [], []
        for h in range(n_heads):
            sl = slice(h * HEAD_DIM, (h + 1) * HEAD_DIM)
            s = lax.dot_general(q_ref[:, sl], k_ref[:, sl], (((1,), (1,)), ((), ())),
                                preferred_element_type=F32) * scale
            s = s + mbias if mode == "far" else s + (mbias + bias_block(mode, h))
            m_new = jnp.maximum(m_all[h], jnp.max(s, axis=1, keepdims=True))
            alpha = jnp.exp(m_all[h] - m_new)
            p = jnp.exp(s - m_new)
            l_out.append(alpha * l_all[h] + jnp.sum(p, axis=1, keepdims=True))
            acc_out.append(alpha * acc_all[h] + jnp.dot(p.astype(BF16), v_ref[:, sl],
                                                        preferred_element_type=F32))
            m_out.append(m_new)
        m_ref[...] = jnp.stack(m_out)
        l_ref[...] = jnp.stack(l_out)
        acc_ref[...] = jnp.stack(acc_out)

    @pl.when(i == j)
    def _():
        run("diag")

    @pl.when(i == j + 1)
    def _():
        run("sub")

    @pl.when(i > j + 1)
    def _():
        run("far")

    @pl.when(i == j)
    def _():
        for h in range(n_heads):
            o_ref[:, h * HEAD_DIM:(h + 1) * HEAD_DIM] = (acc_ref[h] / l_ref[h]).astype(o_ref.dtype)


def _prompt_attention(q, k, v, mask, bias_tiles):
    s, width = q.shape
    n_heads = width // HEAD_DIM
    blk = _tile(s, 512)
    nblk = s // blk
    ii = np.concatenate([np.full(i + 1, i, np.int32) for i in range(nblk)])
    jj = np.concatenate([np.arange(i + 1, dtype=np.int32) for i in range(nblk)])
    kern = functools.partial(_attn_kernel, n_heads=n_heads, blk=blk)
    grid_spec = pltpu.PrefetchScalarGridSpec(
        num_scalar_prefetch=2,
        grid=(len(ii),),
        in_specs=[pl.BlockSpec((blk, width), lambda t, ii, jj: (ii[t], 0)),
                  pl.BlockSpec((blk, width), lambda t, ii, jj: (jj[t], 0)),
                  pl.BlockSpec((blk, width), lambda t, ii, jj: (jj[t], 0)),
                  pl.BlockSpec((blk, blk), lambda t, ii, jj: (ii[t], jj[t])),
                  pl.BlockSpec((2, n_heads, LANES, LANES), lambda t, ii, jj: (0, 0, 0, 0))],
        out_specs=pl.BlockSpec((blk, width), lambda t, ii, jj: (ii[t], 0)),
        scratch_shapes=[pltpu.VMEM((n_heads, blk, 1), F32),
                        pltpu.VMEM((n_heads, blk, 1), F32),
                        pltpu.VMEM((n_heads, blk, HEAD_DIM), F32)],
    )
    return pl.pallas_call(
        kern,
        grid_spec=grid_spec,
        out_shape=jax.ShapeDtypeStruct((s, width), BF16),
        compiler_params=_params("arbitrary"),
        name="prompt_attention",
    )(jnp.asarray(ii), jnp.asarray(jj), q, k, v, mask, bias_tiles)


def _sidx_kernel(pt_ref, iq_ref, iw_ref, cache_ref, o_ref, buf_ref, sem_ref, *, chunk_pages):
    b = pl.program_id(0)
    nb = pl.num_programs(0)
    n_pages, page = buf_ref.shape[1], buf_ref.shape[2]

    def page_copy(bb, slot, p):
        return pltpu.make_async_copy(cache_ref.at[0, pt_ref[bb, p]], buf_ref.at[slot, p], sem_ref.at[slot])

    def start_all(bb, slot):
        def body(p, carry):
            page_copy(bb, slot, p).start()
            return carry
        lax.fori_loop(0, n_pages, body, 0)

    @pl.when(b == 0)
    def _():
        start_all(0, 0)

    @pl.when(b + 1 < nb)
    def _():
        start_all(b + 1, (b + 1) % 2)

    slot = b % 2

    def wait_body(p, carry):
        page_copy(b, slot, p).wait()
        return carry
    lax.fori_loop(0, n_pages, wait_body, 0)

    iq = iq_ref[0]
    w = iw_ref[0]
    ck = chunk_pages * page

    def chunk(c, carry):
        p0 = pl.multiple_of(c * chunk_pages, chunk_pages)
        kb = buf_ref[slot, pl.ds(p0, chunk_pages)].reshape(ck, IDX_DIM).astype(BF16)
        d = lax.dot_general(iq, kb, (((1,), (1,)), ((), ())), preferred_element_type=F32)
        sc = jnp.sum(jnp.maximum(d, 0.0) * w, axis=0, keepdims=True) + 0.0
        o_ref[0, :, pl.ds(pl.multiple_of(c * ck, ck), ck)] = sc
        return carry
    lax.fori_loop(0, n_pages // chunk_pages, chunk, 0)


def _sample_scores(iq, iw, cache_idx_k, page_table):
    nb, n_pages = page_table.shape
    page = cache_idx_k.shape[2]
    chunk_pages = _tile(n_pages, 8)
    kern = functools.partial(_sidx_kernel, chunk_pages=chunk_pages)
    grid_spec = pltpu.PrefetchScalarGridSpec(
        num_scalar_prefetch=1,
        grid=(nb,),
        in_specs=[pl.BlockSpec((1, N_IDX_HEADS, IDX_DIM), lambda b, pt: (b, 0, 0)),
                  pl.BlockSpec((1, N_IDX_HEADS, 1), lambda b, pt: (b, 0, 0)),
                  pl.BlockSpec(memory_space=pl.ANY)],
        out_specs=pl.BlockSpec((1, 1, n_pages * page), lambda b, pt: (b, 0, 0)),
        scratch_shapes=[pltpu.VMEM((2, n_pages, page, IDX_DIM), F32),
                        pltpu.SemaphoreType.DMA((2,))],
    )
    return pl.pallas_call(
        kern,
        grid_spec=grid_spec,
        out_shape=jax.ShapeDtypeStruct((nb, 1, n_pages * page), F32),
        compiler_params=_params("arbitrary"),
        name="sample_scores",
    )(page_table, iq, iw, cache_idx_k)


def _sthr_kernel(sc_ref, iq_ref, iw_ref, ikn_ref, idx_ref, nkeep_ref, seln_ref, key_ref, slot_ref,
                 *, n_sel, n_slot, tk):
    nb, past = sc_ref.shape
    nc = past // tk
    nlane = tk // LANES

    prod = iq_ref[...].astype(F32) * ikn_ref[...].astype(F32)
    dn = jnp.sum(prod, axis=2)
    sn = jnp.sum(jnp.maximum(dn, 0.0) * iw_ref[...], axis=1, keepdims=True) + 0.0
    kn = _sort_key(jnp.broadcast_to(sn, (nb, LANES)))

    def to_keys(c, carry):
        k0 = pl.multiple_of(c * tk, tk)
        key_ref[:, pl.ds(k0, tk)] = _sort_key(sc_ref[:, pl.ds(k0, tk)])
        return carry
    lax.fori_loop(0, nc, to_keys, 0)

    def count_ge(cand):
        def body(c, cnt):
            k0 = pl.multiple_of(c * tk, tk)
            kk = key_ref[:, pl.ds(k0, tk)]
            for j in range(nlane):
                cnt = cnt + jnp.where(kk[:, j * LANES:(j + 1) * LANES] >= cand, 1, 0)
            return cnt
        cnt = lax.fori_loop(0, nc, body, jnp.zeros((nb, LANES), I32))
        return (jnp.sum(cnt.astype(F32), axis=1, keepdims=True)
                + jnp.where(kn[:, :1] >= cand[:, :1], 1.0, 0.0))

    thr = _bisect_threshold(count_ge, nb, n_sel)
    cgt = count_ge(thr + 1)
    need = n_sel - cgt
    upper = (lax.broadcasted_iota(I32, (tk, tk), 0)
             < lax.broadcasted_iota(I32, (tk, tk), 1)).astype(BF16)
    slot_iota = lax.broadcasted_iota(I32, (n_slot, tk), 0).astype(F32)
    lane_iota = lax.broadcasted_iota(I32, (n_slot, tk), 1).astype(F32)
    slot_ref[...] = jnp.zeros(slot_ref.shape, F32)

    def body(c, carry):
        n_eq, n_kept = carry
        k0 = pl.multiple_of(c * tk, tk)
        kk = key_ref[:, pl.ds(k0, tk)]
        thr_t = jnp.concatenate([thr] * nlane, axis=1)
        eq = kk == thr_t
        eqf = jnp.where(eq, 1.0, 0.0)
        before = jnp.dot(eqf.astype(BF16), upper, preferred_element_type=F32) + n_eq
        keep = jnp.logical_or(kk > thr_t, jnp.logical_and(eq, before < need))
        keepf = jnp.where(keep, 1.0, 0.0)
        rank = jnp.dot(keepf.astype(BF16), upper, preferred_element_type=F32) + n_kept
        rank = jnp.where(keep, rank, -1.0)
        pos = lane_iota + k0.astype(F32)
        for b in range(nb):
            hit = jnp.broadcast_to(rank[b:b + 1, :], (n_slot, tk)) == slot_iota
            slot_ref[b] += jnp.where(hit, pos, 0.0)
        return (n_eq + jnp.sum(eqf, axis=1, keepdims=True),
                n_kept + jnp.sum(keepf, axis=1, keepdims=True))
    zero = jnp.zeros((nb, 1), F32)
    n_eq, n_kept = lax.fori_loop(0, nc, body, (zero, zero))
    for b in range(nb):
        idx_ref[b] = jnp.sum(slot_ref[b], axis=1, keepdims=True).astype(I32)
    nkeep_ref[...] = jnp.broadcast_to(n_kept, (nb, LANES))
    keep_new = jnp.logical_or(kn > thr, jnp.logical_and(kn == thr, n_eq < need))
    seln_ref[...] = jnp.where(keep_new, 1.0, 0.0)


def _sample_select(scores, iq, iw, ik_new, n_sel, n_slot):
    nb, past = scores.shape
    tk = _tile(past, 256)
    kern = functools.partial(_sthr_kernel, n_sel=n_sel, n_slot=n_slot, tk=tk)
    return pl.pallas_call(
        kern,
        out_shape=[jax.ShapeDtypeStruct((nb, n_slot, 1), I32),
                   jax.ShapeDtypeStruct((nb, LANES), F32),
                   jax.ShapeDtypeStruct((nb, LANES), F32)],
        scratch_shapes=[pltpu.VMEM((nb, past), I32),
                        pltpu.VMEM((nb, n_slot, tk), F32)],
        compiler_params=_params(),
        name="sample_select",
    )(scores, iq, iw, ik_new)


def _sattn_kernel(pt_ref, idx_ref, q_ref, ck_ref, cv_ref, pos_ref, nkeep_ref, kn_ref, vn_ref, seln_ref,
                  rb_ref, o_ref, kbuf_ref, vbuf_ref, sem_ref, *, n_heads, n_slot, page, past, rows):
    b = pl.program_id(0)
    nb = pl.num_programs(0)
    scale = HEAD_DIM ** -0.5

    def row_copies(bb, slot, r):
        s = idx_ref[bb, r]
        phys = pt_ref[bb, s // page]
        off = s % page
        dst = pl.ds(r * n_heads, n_heads)
        return (pltpu.make_async_copy(ck_ref.at[0, phys, off], kbuf_ref.at[slot, dst], sem_ref.at[0, slot]),
                pltpu.make_async_copy(cv_ref.at[0, phys, off], vbuf_ref.at[slot, dst], sem_ref.at[1, slot]))

    def start_all(bb, slot):
        def body(r, carry):
            ck, cv = row_copies(bb, slot, r)
            ck.start()
            cv.start()
            return carry
        lax.fori_loop(0, n_slot, body, 0)

    @pl.when(b == 0)
    def _():
        start_all(0, 0)

    @pl.when(b + 1 < nb)
    def _():
        start_all(b + 1, (b + 1) % 2)

    slot = b % 2

    def wait_body(r, carry):
        ck, cv = row_copies(b, slot, r)
        ck.wait()
        cv.wait()
        return carry
    lax.fori_loop(0, n_slot, wait_body, 0)

    pos = pos_ref[0]
    bucket = _t5_bucket(past - pos)
    valid = lax.broadcasted_iota(I32, (1, n_slot), 1).astype(F32) < nkeep_ref[0, :, 0:1]
    vnew = seln_ref[0, :, 0:1] > 0.0
    for h in range(n_heads):
        sl = slice(h * HEAD_DIM, (h + 1) * HEAD_DIM)
        bias = jnp.zeros((1, n_slot), F32)
        for bk in range(N_BUCKETS):
            bias = jnp.where(bucket == bk, rb_ref[bk, h], bias)
        qh = q_ref[0, :, sl]
        kh = kbuf_ref[slot, pl.ds(h, n_slot, stride=n_heads), :].astype(BF16)
        vh = vbuf_ref[slot, pl.ds(h, n_slot, stride=n_heads), :].astype(BF16)
        s = lax.dot_general(jnp.broadcast_to(qh, (rows, HEAD_DIM)), kh, (((1,), (1,)), ((), ())),
                            preferred_element_type=F32) * scale
        s = jnp.where(valid, s + bias, -jnp.inf)
        knh = kn_ref[0, :, sl].astype(BF16).astype(F32)
        vnh = vn_ref[0, :, sl].astype(BF16).astype(F32)
        sn = jnp.sum(qh.astype(F32) * knh, axis=1, keepdims=True) * scale + rb_ref[0, h]
        sn = jnp.where(vnew, sn, -jnp.inf)
        m = jnp.maximum(jnp.max(s, axis=1, keepdims=True), sn)
        pr = jnp.exp(s - m)
        pn = jnp.exp(sn - m)
        l = jnp.sum(pr, axis=1, keepdims=True) + pn
        acc = (jnp.dot(pr.astype(BF16), vh, preferred_element_type=F32)
               + pn.astype(BF16).astype(F32) * vnh)
        o_ref[0, :, sl] = (acc[0:1] / l[0:1]).astype(o_ref.dtype)


def _sample_attention(q, k_new, v_new, cache_k, cache_v, idx, nkeep, sel_new, page_table, rel_bias):
    nb, n_pages = page_table.shape
    _, n_phys, page, n_heads, _ = cache_k.shape
    width = n_heads * HEAD_DIM
    n_slot = idx.shape[1]
    rows = 16
    kern = functools.partial(_sattn_kernel, n_heads=n_heads, n_slot=n_slot, page=page,
                             past=n_pages * page, rows=rows)
    per_b = lambda b, pt, ix: (b, 0, 0)
    grid_spec = pltpu.PrefetchScalarGridSpec(
        num_scalar_prefetch=2,
        grid=(nb,),
        in_specs=[pl.BlockSpec((1, 1, width), per_b),
                  pl.BlockSpec(memory_space=pl.ANY),
                  pl.BlockSpec(memory_space=pl.ANY),
                  pl.BlockSpec((1, 1, n_slot), per_b),
                  pl.BlockSpec((1, 1, LANES), per_b),
                  pl.BlockSpec((1, 1, width), per_b),
                  pl.BlockSpec((1, 1, width), per_b),
                  pl.BlockSpec((1, 1, LANES), per_b),
                  pl.BlockSpec(memory_space=pltpu.SMEM)],
        out_specs=pl.BlockSpec((1, 1, width), per_b),
        scratch_shapes=[pltpu.VMEM((2, n_slot * n_heads, HEAD_DIM), F32),
                        pltpu.VMEM((2, n_slot * n_heads, HEAD_DIM), F32),
                        pltpu.SemaphoreType.DMA((2, 2))],
    )
    return pl.pallas_call(
        kern,
        grid_spec=grid_spec,
        out_shape=jax.ShapeDtypeStruct((nb, 1, width), BF16),
        compiler_params=_params("arbitrary"),
        name="sample_attention",
    )(page_table, idx, q, cache_k, cache_v, idx.reshape(nb, 1, n_slot), nkeep.reshape(nb, 1, LANES),
      k_new, v_new, sel_new.reshape(nb, 1, LANES), rel_bias)


def _pool_kernel(u_ref, prev_ref, w_ref, ps_ref, o_ref, *, tm, pos0):
    i = pl.program_id(1)
    cur = u_ref[0]
    prev = jnp.where(i == 0, 0.0, prev_ref[0])
    ext = jnp.concatenate([prev, cur], axis=0)
    gdim = cur.shape[1] // len(POOL_WINDOWS)
    t = pos0 + i * tm + lax.broadcasted_iota(I32, (tm, 1), 0)
    for g, win in enumerate(POOL_WINDOWS):
        cs = slice(g * gdim, (g + 1) * gdim)
        s = ext[:, cs]
        span = 1
        while span < win:
            s = s[span:] + s[:-span]
            span *= 2
        wsum = s[s.shape[0] - tm:]
        cnt = jnp.minimum(t + 1, win).astype(F32)
        pooled = (wsum / cnt - cur[:, cs]).astype(BF16)
        mixed = jnp.dot(pooled, w_ref[g], preferred_element_type=F32)
        o_ref[0, :, cs] = (mixed * ps_ref[:, cs]).astype(o_ref.dtype)


def _pool(u, w_pool, pool_scale, pos0):
    nb, n, width = u.shape
    halo = POOL_CTX + 1
    tm = _tile(n, 512)
    ratio = tm // halo
    kern = functools.partial(_pool_kernel, tm=tm, pos0=pos0)
    return pl.pallas_call(
        kern,
        grid=(nb, n // tm),
        in_specs=[pl.BlockSpec((1, tm, width), lambda b, i: (b, i, 0)),
                  pl.BlockSpec((1, halo, width), lambda b, i: (b, jnp.maximum(i * ratio - 1, 0), 0)),
                  pl.BlockSpec(w_pool.shape, lambda b, i: (0, 0, 0)),
                  pl.BlockSpec((1, width), lambda b, i: (0, 0))],
        out_specs=pl.BlockSpec((1, tm, width), lambda b, i: (b, i, 0)),
        out_shape=jax.ShapeDtypeStruct((nb, n, width), BF16),
        compiler_params=_params("arbitrary", "arbitrary"),
        name="pool",
    )(u, u, w_pool, pool_scale)


def _gateproj_kernel(a_ref, p_ref, wa_ref, wp_ref, ga_ref, gb_ref, o_ref):
    a = jnp.dot(a_ref[...], wa_ref[...], preferred_element_type=F32)
    p = jnp.dot(p_ref[...], wp_ref[...], preferred_element_type=F32)
    o_ref[...] = (_sigmoid(ga_ref[...]) * a + _sigmoid(gb_ref[...]) * p).astype(o_ref.dtype)


def _gateproj(attn, pooled, wa, wp, ga, gb):
    m, ka = attn.shape
    kp = pooled.shape[1]
    n = wa.shape[1]
    tm = _tile(m, 512)
    tn = _tile(n, 512)
    return pl.pallas_call(
        _gateproj_kernel,
        grid=(m // tm, n // tn),
        in_specs=[pl.BlockSpec((tm, ka), lambda i, j: (i, 0)),
                  pl.BlockSpec((tm, kp), lambda i, j: (i, 0)),
                  pl.BlockSpec((ka, tn), lambda i, j: (0, j)),
                  pl.BlockSpec((kp, tn), lambda i, j: (0, j)),
                  pl.BlockSpec((tm, tn), lambda i, j: (i, j)),
                  pl.BlockSpec((tm, tn), lambda i, j: (i, j))],
        out_specs=pl.BlockSpec((tm, tn), lambda i, j: (i, j)),
        out_shape=jax.ShapeDtypeStruct((m, n), BF16),
        compiler_params=_params("arbitrary", "arbitrary"),
        name="gateproj",
    )(attn, pooled, wa, wp, ga, gb)


def _mixout_kernel(m_ref, w_ref, x_ref, gate_ref, gpost_ref, gpre_ref, sc_ref, sh_ref, x1_ref, h2_ref):
    y = jnp.dot(m_ref[...], w_ref[...], preferred_element_type=F32)
    x1 = x_ref[...] + gate_ref[...] * _rms(y, gpost_ref[...])
    x1_ref[...] = x1
    h2_ref[...] = (_rms(x1, gpre_ref[...]) * (1 + sc_ref[...]) + sh_ref[...]).astype(h2_ref.dtype)


def _mixout(m, w_out, x, gate1, g_post, g_pre_ffn, scale2, shift2):
    t, d = x.shape
    tm = _tile(t, 256)
    row = pl.BlockSpec((1, d), lambda i: (0, 0))
    tok = pl.BlockSpec((tm, d), lambda i: (i, 0))
    return pl.pallas_call(
        _mixout_kernel,
        grid=(t // tm,),
        in_specs=[tok, pl.BlockSpec((d, d), lambda i: (0, 0)), tok, _mod_spec(gate1, tm, d),
                  row, row, _mod_spec(scale2, tm, d), _mod_spec(shift2, tm, d)],
        out_specs=[tok, tok],
        out_shape=[jax.ShapeDtypeStruct((t, d), F32), jax.ShapeDtypeStruct((t, d), BF16)],
        compiler_params=_params("arbitrary"),
        name="mixout",
    )(m, w_out, x, gate1, g_post, g_pre_ffn, scale2, shift2)


def _router_kernel(h_ref, wt_ref, b_ref, o_ref):
    n_exp, tm = o_ref.shape
    per_group = n_exp // N_GROUP
    logits = lax.dot_general(wt_ref[...], h_ref[...], (((1,), (1,)), ((), ())),
                             preferred_element_type=F32)
    s = _sigmoid(logits)
    sb = s + b_ref[...]
    gidx = lax.broadcasted_iota(I32, (per_group, tm), 0)
    gscores = []
    for g in range(N_GROUP):
        x = sb[g * per_group:(g + 1) * per_group]
        m1 = jnp.max(x, axis=0, keepdims=True)
        i1 = jnp.min(jnp.where(x == m1, gidx, per_group), axis=0, keepdims=True)
        m2 = jnp.max(jnp.where(gidx == i1, -jnp.inf, x), axis=0, keepdims=True)
        gscores.append(m1 + m2)
    cur = jnp.concatenate(gscores, axis=0)
    ridx = lax.broadcasted_iota(I32, (N_GROUP, tm), 0)
    gsel = jnp.zeros((N_GROUP, tm), F32)
    for _ in range(TOPK_GROUP):
        m = jnp.max(cur, axis=0, keepdims=True)
        im = jnp.min(jnp.where(cur == m, ridx, N_GROUP), axis=0, keepdims=True)
        pick = ridx == im
        gsel = jnp.where(pick, 1.0, gsel)
        cur = jnp.where(pick, -jnp.inf, cur)
    emask = jnp.concatenate(
        [jnp.broadcast_to(gsel[g:g + 1], (per_group, tm)) for g in range(N_GROUP)], axis=0)
    cur = jnp.where(emask > 0.0, sb, -jnp.inf)
    eidx = lax.broadcasted_iota(I32, (n_exp, tm), 0)
    esel = jnp.zeros((n_exp, tm), F32)
    for _ in range(TOP_K):
        m = jnp.max(cur, axis=0, keepdims=True)
        im = jnp.min(jnp.where(cur == m, eidx, n_exp), axis=0, keepdims=True)
        pick = eidx == im
        esel = jnp.where(pick, 1.0, esel)
        cur = jnp.where(pick, -jnp.inf, cur)
    w = jnp.where(esel > 0.0, s, 0.0)
    o_ref[...] = w / jnp.sum(w, axis=0, keepdims=True) * ROUTED_SCALE


def _router(h, w_router_t, b_router):
    t, d = h.shape
    n_exp = w_router_t.shape[0]
    tm = _tile(t, 512)
    return pl.pallas_call(
        _router_kernel,
        grid=(t // tm,),
        in_specs=[pl.BlockSpec((tm, d), lambda i: (i, 0)),
                  pl.BlockSpec((n_exp, d), lambda i: (0, 0)),
                  pl.BlockSpec((n_exp, 1), lambda i: (0, 0))],
        out_specs=pl.BlockSpec((n_exp, tm), lambda i: (0, i)),
        out_shape=jax.ShapeDtypeStruct((n_exp, t), F32),
        compiler_params=_params("arbitrary"),
        name="router",
    )(h, w_router_t, b_router)


def _moe_kernel(h_ref, g_ref, wg_ref, wu_ref, wd_ref, init_ref, x_ref, gate_ref, gpost_ref,
                o_ref, acc_ref, *, final):
    e = pl.program_id(1)
    n_exp = pl.num_programs(1)

    @pl.when(e == 0)
    def _():
        acc_ref[...] = init_ref[...]

    gates = g_ref[...]
    lane = lax.broadcasted_iota(I32, gates.shape, 1)
    gcol = jnp.sum(jnp.where(lane == e, gates, 0.0), axis=1, keepdims=True)

    @pl.when(jnp.max(jnp.abs(gcol)) > 0.0)
    def _():
        hh = h_ref[...]
        a = jnp.dot(hh, wg_ref[0], preferred_element_type=F32)
        b = jnp.dot(hh, wu_ref[0], preferred_element_type=F32)
        act = (a * _sigmoid(a) * b).astype(BF16)
        y = jnp.dot(act, wd_ref[0], preferred_element_type=F32)
        acc_ref[...] += y * gcol

    @pl.when(e == n_exp - 1)
    def _():
        if final:
            o_ref[...] = x_ref[...] + gate_ref[...] * _rms(acc_ref[...], gpost_ref[...])
        else:
            o_ref[...] = acc_ref[...]


def _moe(h, gates, wg, wu, wd, init, x, gate2, g_post, final):
    t, d = h.shape
    n_exp, _, f = wg.shape
    tm = _tile(t, 512)
    tok = pl.BlockSpec((tm, d), lambda i, e: (i, 0))
    kern = functools.partial(_moe_kernel, final=final)
    return pl.pallas_call(
        kern,
        grid=(t // tm, n_exp),
        in_specs=[tok,
                  pl.BlockSpec((tm, gates.shape[1]), lambda i, e: (i, 0)),
                  pl.BlockSpec((1, d, f), lambda i, e: (e, 0, 0)),
                  pl.BlockSpec((1, d, f), lambda i, e: (e, 0, 0)),
                  pl.BlockSpec((1, f, d), lambda i, e: (e, 0, 0)),
                  tok, tok, _mod_spec(gate2, tm, d),
                  pl.BlockSpec((1, d), lambda i, e: (0, 0))],
        out_specs=tok,
        out_shape=jax.ShapeDtypeStruct((t, d), F32),
        scratch_shapes=[pltpu.VMEM((tm, d), F32)],
        compiler_params=_params("arbitrary", "arbitrary"),
        name="moe_final" if final else "moe_shared",
    )(h, gates, wg, wu, wd, init, x, gate2, g_post)


def _token_path(x, mods, wts):
    sh1, sc1 = mods[0], mods[1]
    h = _normmod(x, wts["g_pre_mix"], sc1, sh1)
    q = _mm(h, wts["w_q"], BF16)
    k = _mm(h, wts["w_k"], F32)
    v = _mm(h, wts["w_v"], F32)
    iq = _mm(h, wts["w_iq"], BF16)
    u = _mm(h, wts["w_u"], F32)
    ga = _mm(h, wts["w_ga"], F32)
    gb = _mm(h, wts["w_gb"], F32)
    ik, iw = _idxproj(h, wts["w_idx"], wts["idx_k_norm"])
    return q, k, v, iq, iw, ik, u, ga, gb


def _back_half(x, attn, pooled, ga, gb, mods, wts):
    _, _, g1, sh2, sc2, g2 = mods
    m = _gateproj(attn, pooled, wts["w_proj_attn"], wts["w_proj_pool"], ga, gb)
    x1, h2 = _mixout(m, wts["w_out"], x, g1, wts["g_post_mix"], wts["g_pre_ffn"], sc2, sh2)
    gates = _router(h2, wts["w_router_t"], wts["b_router"]).T
    t, d = x.shape
    ones = jnp.ones((t, 1), F32)
    zeros = jnp.zeros((t, d), F32)
    shared = _moe(h2, ones, wts["w_gate_s"], wts["w_up_s"], wts["w_down_s"], zeros, x1, g2,
                  wts["g_post_ffn"], final=False)
    return _moe(h2, gates, wts["w_gate_e"], wts["w_up_e"], wts["w_down_e"], shared, x1, g2,
                wts["g_post_ffn"], final=True)


def kernel(x_prompt, x_sample, c_prompt, c_sample, cache_k, cache_v, cache_idx_k, state_pool, page_table, w_ada, b_ada, g_pre_mix, w_in, idx_k_norm, rel_bias, w_pool, pool_scale, w_proj_attn, w_proj_pool, w_out, g_post_mix, g_pre_ffn, w_router, b_router, w_gate_e, w_up_e, w_down_e, w_gate_s, w_up_s, w_down_s, g_post_ffn):
    depth = w_ada.shape[0]
    assert depth == 1, "single-layer trunk"
    nbp, seq, d = x_prompt.shape
    assert nbp == 1, "one prompt sequence"
    nbs, dec_seq, _ = x_sample.shape
    assert dec_seq == 1, "one new token per sample sequence"
    _, n_phys, page, n_heads, head_dim = cache_k.shape
    assert head_dim == HEAD_DIM and page == LANES
    aw = n_heads * HEAD_DIM
    pw = state_pool.shape[-1]
    n_pages = page_table.shape[1]
    past = n_pages * page
    iqw = N_IDX_HEADS * IDX_DIM

    win = w_in[0]
    offs = np.cumsum([0, aw, aw, aw, iqw, IDX_DIM, N_IDX_HEADS, pw, d, d])
    seg = lambda a: win[:, offs[a]:offs[a + 1]].astype(BF16)
    w_idx = jnp.pad(win[:, offs[4]:offs[6]], ((0, 0), (0, LANES - IDX_DIM - N_IDX_HEADS))).astype(BF16)
    wts = dict(
        g_pre_mix=g_pre_mix, idx_k_norm=idx_k_norm,
        w_q=seg(0), w_k=seg(1), w_v=seg(2), w_iq=seg(3), w_idx=w_idx, w_u=seg(6), w_ga=seg(7), w_gb=seg(8),
        w_proj_attn=w_proj_attn[0].astype(BF16), w_proj_pool=w_proj_pool[0].astype(BF16),
        w_out=w_out[0].astype(BF16), g_post_mix=g_post_mix, g_pre_ffn=g_pre_ffn,
        w_router_t=w_router[0].T.astype(BF16), b_router=b_router.reshape(-1, 1),
        w_gate_e=w_gate_e[0].astype(BF16), w_up_e=w_up_e[0].astype(BF16), w_down_e=w_down_e[0].astype(BF16),
        w_gate_s=w_gate_s.astype(BF16), w_up_s=w_up_s.astype(BF16), w_down_s=w_down_s.astype(BF16),
        g_post_ffn=g_post_ffn,
    )
    w_pool_b = w_pool[0].astype(BF16)

    n_c = nbp + nbs
    rows = -(-n_c // SUBLANES) * SUBLANES
    c_all = jnp.pad(jnp.concatenate([c_prompt, c_sample], axis=0), ((0, rows - n_c), (0, 0)))
    mod = _ada(c_all, w_ada[0], b_ada)
    mods_p = [mod[0:1, a * d:(a + 1) * d] for a in range(6)]
    mods_s = [mod[nbp:n_c, a * d:(a + 1) * d] for a in range(6)]

    bias_tiles = _bias_tiles(rel_bias)

    xp = x_prompt.reshape(seq, d)
    q, k, v, iq, iw, ik, u, ga, gb = _token_path(xp, mods_p, wts)
    mask = _prompt_select(iq, ik.astype(BF16).T, iw, min(TOPK_MAX, seq // 4))
    attn = _prompt_attention(q, k.astype(BF16), v.astype(BF16), mask, bias_tiles)
    pooled = _pool(u.reshape(1, seq, pw), w_pool_b, pool_scale, 0).reshape(seq, pw)
    y_prompt = _back_half(xp, attn, pooled, ga, gb, mods_p, wts).reshape(nbp, seq, d)
    k_prompt = k.reshape(1, nbp, seq, n_heads, HEAD_DIM)
    v_prompt = v.reshape(1, nbp, seq, n_heads, HEAD_DIM)
    idxk_prompt = ik.reshape(1, nbp, seq, IDX_DIM)
    pool_prompt = u[seq - POOL_CTX:].reshape(1, nbp, POOL_CTX, pw)

    xs = x_sample.reshape(nbs, d)
    qs, ks, vs, iqs, iws, iks, us, gas, gbs = _token_path(xs, mods_s, wts)
    iq3 = iqs.reshape(nbs, N_IDX_HEADS, IDX_DIM)
    iw3 = iws.reshape(nbs, N_IDX_HEADS, 1)
    scores = _sample_scores(iq3, iw3, cache_idx_k, page_table).reshape(nbs, past)
    n_sel_s = min(TOPK_MAX, (past + dec_seq) // 4)
    n_slot = -(-n_sel_s // LANES) * LANES
    idx, nkeep, sel_new = _sample_select(scores, iq3, iws, iks.astype(BF16).reshape(nbs, 1, IDX_DIM),
                                         n_sel_s, n_slot)
    attn_s = _sample_attention(
        qs.reshape(nbs, 1, aw), ks.reshape(nbs, 1, aw), vs.reshape(nbs, 1, aw), cache_k, cache_v,
        idx.reshape(nbs, n_slot), nkeep, sel_new, page_table, rel_bias).reshape(nbs, aw)
    ext = jnp.concatenate([state_pool[0], us.reshape(nbs, 1, pw)], axis=1)
    pooled_s = _pool(ext, w_pool_b, pool_scale, past - POOL_CTX)[:, POOL_CTX]
    y_sample = _back_half(xs, attn_s, pooled_s, gas, gbs, mods_s, wts).reshape(nbs, 1, d)
    k_sample = ks.reshape(1, nbs, 1, n_heads, HEAD_DIM)
    v_sample = vs.reshape(1, nbs, 1, n_heads, HEAD_DIM)
    idxk_sample = iks.reshape(1, nbs, 1, IDX_DIM)
    pool_sample = ext[:, 1:].reshape(1, nbs, POOL_CTX, pw)

    return (y_prompt, y_sample, k_prompt, v_prompt, idxk_prompt, pool_prompt,
            k_sample, v_sample, idxk_sample, pool_sample)
```

```python
import functools
import math

import jax
import jax.numpy as jnp
import numpy as np
from jax import lax
from jax.experimental import pallas as pl
from jax.experimental.pallas import tpu as pltpu

F32 = jnp.float32
BF16 = jnp.bfloat16
I32 = jnp.int32

HEAD_DIM = 128
N_IDX_HEADS = 16
IDX_DIM = 64
TOPK_MAX = 256
N_BUCKETS = 32
MAX_DISTANCE = 128
POOL_WINDOWS = (2, 4, 8, 16)
POOL_CTX = max(POOL_WINDOWS) - 1
N_GROUP = 8
TOPK_GROUP = 4
TOP_K = 8
ROUTED_SCALE = 2.5
EPS = 1e-6

LANES = 128
SUBLANES = 8
VMEM_LIMIT_BYTES = 56 * 1024 * 1024

NEG_BIG = -1e30
NEG_INF_KEY = int(np.int32(np.uint32(0xFF800000)) ^ np.int32(0x7FFFFFFF))
INT_MIN = -(2 ** 31)


def _params(*sem):
    return pltpu.CompilerParams(dimension_semantics=sem, vmem_limit_bytes=VMEM_LIMIT_BYTES)


def _tile(n, pref):
    if n <= pref:
        return n
    t = pref
    while n % t:
        t //= 2
    return t


def _rms(x, g):
    return x * lax.rsqrt(jnp.mean(x * x, axis=-1, keepdims=True) + EPS) * g


def _sigmoid(x):
    return jax.nn.sigmoid(x)


def _ada_kernel(c_ref, w_ref, b_ref, o_ref):
    c = c_ref[...]
    a = (c * _sigmoid(c)).astype(BF16)
    o_ref[...] = jnp.dot(a, w_ref[...].astype(BF16), preferred_element_type=F32) + b_ref[...]


def _ada(c, w, b):
    r, d = c.shape
    n = w.shape[1]
    tn = _tile(n, 1024)
    return pl.pallas_call(
        _ada_kernel,
        grid=(n // tn,),
        in_specs=[pl.BlockSpec((r, d), lambda j: (0, 0)),
                  pl.BlockSpec((d, tn), lambda j: (0, j)),
                  pl.BlockSpec((1, tn), lambda j: (0, j))],
        out_specs=pl.BlockSpec((r, tn), lambda j: (0, j)),
        out_shape=jax.ShapeDtypeStruct((r, n), F32),
        compiler_params=_params("arbitrary"),
        name="ada",
    )(c, w, b)


def _normmod_kernel(x_ref, g_ref, sc_ref, sh_ref, o_ref):
    y = _rms(x_ref[...], g_ref[...])
    o_ref[...] = (y * (1 + sc_ref[...]) + sh_ref[...]).astype(o_ref.dtype)


def _mod_spec(mod, tm, d):
    if mod.shape[0] == 1:
        return pl.BlockSpec((1, d), lambda i, *_: (0, 0))
    return pl.BlockSpec((tm, d), lambda i, *_: (i, 0))


def _normmod(x, g, scale, shift):
    t, d = x.shape
    tm = _tile(t, 512)
    return pl.pallas_call(
        _normmod_kernel,
        grid=(t // tm,),
        in_specs=[pl.BlockSpec((tm, d), lambda i: (i, 0)),
                  pl.BlockSpec((1, d), lambda i: (0, 0)),
                  _mod_spec(scale, tm, d), _mod_spec(shift, tm, d)],
        out_specs=pl.BlockSpec((tm, d), lambda i: (i, 0)),
        out_shape=jax.ShapeDtypeStruct((t, d), BF16),
        compiler_params=_params("arbitrary"),
        name="normmod",
    )(x, g, scale, shift)


def _mm_kernel(a_ref, w_ref, o_ref):
    o_ref[...] = jnp.dot(a_ref[...], w_ref[...], preferred_element_type=F32).astype(o_ref.dtype)


def _mm(a, w, out_dtype):
    m, k = a.shape
    n = w.shape[1]
    tm = _tile(m, 512)
    tn = _tile(n, 512)
    return pl.pallas_call(
        _mm_kernel,
        grid=(m // tm, n // tn),
        in_specs=[pl.BlockSpec((tm, k), lambda i, j: (i, 0)),
                  pl.BlockSpec((k, tn), lambda i, j: (0, j))],
        out_specs=pl.BlockSpec((tm, tn), lambda i, j: (i, j)),
        out_shape=jax.ShapeDtypeStruct((m, n), out_dtype),
        compiler_params=_params("arbitrary", "arbitrary"),
        name="mm",
    )(a, w)


def _idxproj_kernel(a_ref, w_ref, g_ref, ik_ref, iw_ref):
    z = jnp.dot(a_ref[...], w_ref[...], preferred_element_type=F32)
    ik = z[:, :IDX_DIM]
    xc = ik - jnp.mean(ik, axis=-1, keepdims=True)
    y = xc * lax.rsqrt(jnp.mean(xc * xc, axis=-1, keepdims=True) + EPS)
    ik_ref[...] = y * g_ref[...]
    iw_ref[...] = z[:, IDX_DIM:IDX_DIM + N_IDX_HEADS] * (IDX_DIM ** -0.5 * N_IDX_HEADS ** -0.5)


def _idxproj(a, w, g):
    m, k = a.shape
    tm = _tile(m, 512)
    return pl.pallas_call(
        _idxproj_kernel,
        grid=(m // tm,),
        in_specs=[pl.BlockSpec((tm, k), lambda i: (i, 0)),
                  pl.BlockSpec((k, LANES), lambda i: (0, 0)),
                  pl.BlockSpec((1, IDX_DIM), lambda i: (0, 0))],
        out_specs=[pl.BlockSpec((tm, IDX_DIM), lambda i: (i, 0)),
                   pl.BlockSpec((tm, N_IDX_HEADS), lambda i: (i, 0))],
        out_shape=[jax.ShapeDtypeStruct((m, IDX_DIM), F32),
                   jax.ShapeDtypeStruct((m, N_IDX_HEADS), F32)],
        compiler_params=_params("arbitrary"),
        name="idxproj",
    )(a, w, g)


def _t5_bucket(rel):
    n = jnp.maximum(rel, 0)
    max_exact = N_BUCKETS // 2
    nf = jnp.maximum(n, 1).astype(F32)
    large = max_exact + (jnp.log(nf / max_exact) / math.log(MAX_DISTANCE / max_exact)
                         * (N_BUCKETS - max_exact)).astype(I32)
    large = jnp.minimum(large, N_BUCKETS - 1)
    return jnp.where(n < max_exact, n, large)


def _bias_kernel(rb_ref, o_ref):
    n_heads = o_ref.shape[1]
    r = lax.broadcasted_iota(I32, (LANES, LANES), 0)
    c = lax.broadcasted_iota(I32, (LANES, LANES), 1)
    for d in range(2):
        bucket = _t5_bucket(d * LANES + r - c)
        for h in range(n_heads):
            val = jnp.zeros((LANES, LANES), F32)
            for b in range(N_BUCKETS):
                val = jnp.where(bucket == b, rb_ref[b, h], val)
            o_ref[d, h] = val - rb_ref[N_BUCKETS - 1, h]


def _bias_tiles(rel_bias):
    n_heads = rel_bias.shape[1]
    return pl.pallas_call(
        _bias_kernel,
        in_specs=[pl.BlockSpec(memory_space=pltpu.SMEM)],
        out_shape=jax.ShapeDtypeStruct((2, n_heads, LANES, LANES), F32),
        name="bias_tiles",
    )(rel_bias)


def _sort_key(x):
    bits = pltpu.bitcast(x, I32)
    return jnp.where(bits < 0, bits ^ 0x7FFFFFFF, bits)


def _bisect_threshold(count_ge, rows, n_sel):
    def body(it, prefix):
        bit = 31 - it
        cand = prefix + jnp.left_shift(jnp.int32(1), bit)
        cnt = count_ge(cand)
        return jnp.where(cnt >= n_sel, cand, prefix)

    return lax.fori_loop(0, 32, body, jnp.full((rows, LANES), INT_MIN, I32))


def _idx_kernel(iq_ref, ikt_ref, iw_ref, mask_ref, key_ref, wb_ref, *, tq, tk, ck, rg, n_sel):
    i = pl.program_id(0)
    s_total = ikt_ref.shape[1]
    q0 = i * tq
    nc = (q0 + tq) // tk
    nlane = tk // LANES

    for h in range(N_IDX_HEADS):
        wb_ref[h] = jnp.broadcast_to(iw_ref[:, h:h + 1], (tq, LANES))

    def score_chunk(c, carry):
        k0 = pl.multiple_of(c * tk, tk)
        kt = ikt_ref[:, pl.ds(k0, tk)]
        acc = jnp.zeros((tq, tk), F32)
        for h in range(N_IDX_HEADS):
            d = jnp.dot(iq_ref[:, h * IDX_DIM:(h + 1) * IDX_DIM], kt, preferred_element_type=F32)
            w = jnp.concatenate([wb_ref[h]] * nlane, axis=1)
            acc = acc + jnp.maximum(d, 0.0) * w
        acc = acc + 0.0
        t = q0 + lax.broadcasted_iota(I32, (tq, tk), 0)
        s = k0 + lax.broadcasted_iota(I32, (tq, tk), 1)
        acc = jnp.where(s <= t, acc, -jnp.inf)
        key_ref[:, pl.ds(k0, tk)] = _sort_key(acc)
        return carry

    lax.fori_loop(0, nc, score_chunk, 0)

    ncc = (nc * tk + ck - 1) // ck

    def fill_chunk(c, carry):
        key_ref[:, pl.ds(pl.multiple_of(c * tk, tk), tk)] = jnp.full((tq, tk), NEG_INF_KEY, I32)
        return carry
    lax.fori_loop(nc, ncc * (ck // tk), fill_chunk, 0)

    ones = jnp.ones((LANES, LANES), BF16)

    def count_ge(cand):
        parts = []
        for g in range(tq // rg):
            rows = pl.ds(g * rg, rg)
            cand_g = cand[g * rg:(g + 1) * rg]

            def body(c, cnt, rows=rows, cand_g=cand_g):
                kk = key_ref[rows, pl.ds(pl.multiple_of(c * ck, ck), ck)]
                for j in range(ck // LANES):
                    cnt = cnt + jnp.where(kk[:, j * LANES:(j + 1) * LANES] >= cand_g, 1, 0)
                return cnt
            parts.append(lax.fori_loop(0, ncc, body, jnp.zeros((rg, LANES), I32)))
        cnt = jnp.concatenate(parts, axis=0) if len(parts) > 1 else parts[0]
        return jnp.dot(cnt.astype(F32).astype(BF16), ones, preferred_element_type=F32)

    thr = _bisect_threshold(count_ge, tq, n_sel)
    is_neg = thr == NEG_INF_KEY
    thr_eff = jnp.where(is_neg, NEG_INF_KEY + 1, thr)
    cge = count_ge(thr)
    tied = jnp.logical_and(jnp.logical_not(is_neg), cge > n_sel)
    any_tied = jnp.max(jnp.where(tied, 1.0, 0.0)) > 0.0

    @pl.when(jnp.logical_not(any_tied))
    def _():
        def body(c, carry):
            k0 = pl.multiple_of(c * tk, tk)
            kk = key_ref[:, pl.ds(k0, tk)]
            thr_t = jnp.concatenate([thr_eff] * nlane, axis=1)
            mask_ref[:, pl.ds(k0, tk)] = jnp.where(kk >= thr_t, 1, 0).astype(jnp.int8)
            return carry
        lax.fori_loop(0, nc, body, 0)

    @pl.when(any_tied)
    def _():
        cgt = count_ge(thr + 1)
        need = jnp.where(is_neg[:, :1], 0.0, n_sel - cgt[:, :1])
        upper = (lax.broadcasted_iota(I32, (tk, tk), 0)
                 < lax.broadcasted_iota(I32, (tk, tk), 1)).astype(BF16)

        def body(c, carry):
            k0 = pl.multiple_of(c * tk, tk)
            kk = key_ref[:, pl.ds(k0, tk)]
            thr_t = jnp.concatenate([thr] * nlane, axis=1)
            eq = kk == thr_t
            eqf = jnp.where(eq, 1.0, 0.0)
            before = jnp.dot(eqf.astype(BF16), upper, preferred_element_type=F32) + carry
            keep = jnp.logical_or(kk > thr_t, jnp.logical_and(eq, before < need))
            mask_ref[:, pl.ds(k0, tk)] = jnp.where(keep, 1, 0).astype(jnp.int8)
            return carry + jnp.sum(eqf, axis=1, keepdims=True)
        lax.fori_loop(0, nc, body, jnp.zeros((tq, 1), F32))

    def zero_chunk(c, carry):
        k0 = pl.multiple_of(c * tk, tk)
        mask_ref[:, pl.ds(k0, tk)] = jnp.zeros((tq, tk), jnp.int8)
        return carry
    lax.fori_loop(nc, s_total // tk, zero_chunk, 0)


def _prompt_select(iq, ikt, iw, n_sel):
    s = iq.shape[0]
    tq = _tile(s, 256)
    tk = _tile(tq, 256)
    ck = 2 * tk if s % (2 * tk) == 0 else tk
    assert s // LANES <= 256, "per-lane key counts must stay exact in bf16"
    kern = functools.partial(_idx_kernel, tq=tq, tk=tk, ck=ck, rg=_tile(tq, 64), n_sel=n_sel)
    return pl.pallas_call(
        kern,
        grid=(s // tq,),
        in_specs=[pl.BlockSpec((tq, N_IDX_HEADS * IDX_DIM), lambda i: (i, 0)),
                  pl.BlockSpec((IDX_DIM, s), lambda i: (0, 0)),
                  pl.BlockSpec((tq, N_IDX_HEADS), lambda i: (i, 0))],
        out_specs=pl.BlockSpec((tq, s), lambda i: (i, 0)),
        out_shape=jax.ShapeDtypeStruct((s, s), jnp.int8),
        scratch_shapes=[pltpu.VMEM((tq, s), I32),
                        pltpu.VMEM((N_IDX_HEADS, tq, LANES), F32)],
        compiler_params=_params("arbitrary"),
        name="prompt_select",
    )(iq, ikt, iw)


def _attn_kernel(ii_ref, jj_ref, q_ref, k_ref, v_ref, mask_ref, bt_ref, o_ref,
                 m_ref, l_ref, acc_ref, *, n_heads, blk):
    step = pl.program_id(0)
    i = ii_ref[step]
    j = jj_ref[step]
    nb = blk // LANES
    scale = HEAD_DIM ** -0.5

    @pl.when(j == 0)
    def _():
        m_ref[...] = jnp.full(m_ref.shape, NEG_BIG, F32)
        l_ref[...] = jnp.zeros(l_ref.shape, F32)
        acc_ref[...] = jnp.zeros(acc_ref.shape, F32)

    def bias_block(mode, h):
        zero = jnp.zeros((LANES, LANES), F32)
        rows = []
        for a in range(nb):
            cols = []
            for b in range(nb):
                dd = a - b if mode == "diag" else nb + a - b
                cols.append(bt_ref[0, h] if dd == 0 else bt_ref[1, h] if dd == 1 else zero)
            rows.append(jnp.concatenate(cols, axis=1) if nb > 1 else cols[0])
        return jnp.concatenate(rows, axis=0) if nb > 1 else rows[0]

    def run(mode):
        mbias = jnp.where(mask_ref[...].astype(F32) > 0.0, 0.0, -jnp.inf)
        m_all, l_all, acc_all = m_ref[...], l_ref[...], acc_ref[...]
        m_out, l_out, acc_out = [], [], []
        for h in range(n_heads):
            sl = slice(h * HEAD_DIM, (h + 1) * HEAD_DIM)
            s = lax.dot_general(q_ref[:, sl], k_ref[:, sl], (((1,), (1,)), ((), ())),
                                preferred_element_type=F32) * scale
            s = s + mbias if mode == "far" else s + (mbias + bias_block(mode, h))
            m_new = jnp.maximum(m_all[h], jnp.max(s, axis=1, keepdims=True))
            alpha = jnp.exp(m_all[h] - m_new)
            p = jnp.exp(s - m_new)
            l_out.append(alpha * l_all[h] + jnp.sum(p, axis=1, keepdims=True))
            acc_out.append(alpha * acc_all[h] + jnp.dot(p.astype(BF16), v_ref[:, sl],
                                                        preferred_element_type=F32))
            m_out.append(m_new)
        m_ref[...] = jnp.stack(m_out)
        l_ref[...] = jnp.stack(l_out)
        acc_ref[...] = jnp.stack(acc_out)

    @pl.when(i == j)
    def _():
        run("diag")

    @pl.when(i == j + 1)
    def _():
        run("sub")

    @pl.when(i > j + 1)
    def _():
        run("far")

    @pl.when(i == j)
    def _():
        for h in range(n_heads):
            o_ref[:, h * HEAD_DIM:(h + 1) * HEAD_DIM] = (acc_ref[h] / l_ref[h]).astype(o_ref.dtype)


def _prompt_attention(q, k, v, mask, bias_tiles):
    s, width = q.shape
    n_heads = width // HEAD_DIM
    blk = _tile(s, 512)
    nblk = s // blk
    ii = np.concatenate([np.full(i + 1, i, np.int32) for i in range(nblk)])
    jj = np.concatenate([np.arange(i + 1, dtype=np.int32) for i in range(nblk)])
    kern = functools.partial(_attn_kernel, n_heads=n_heads, blk=blk)
    grid_spec = pltpu.PrefetchScalarGridSpec(
        num_scalar_prefetch=2,
        grid=(len(ii),),
        in_specs=[pl.BlockSpec((blk, width), lambda t, ii, jj: (ii[t], 0)),
                  pl.BlockSpec((blk, width), lambda t, ii, jj: (jj[t], 0)),
                  pl.BlockSpec((blk, width), lambda t, ii, jj: (jj[t], 0)),
                  pl.BlockSpec((blk, blk), lambda t, ii, jj: (ii[t], jj[t])),
                  pl.BlockSpec((2, n_heads, LANES, LANES), lambda t, ii, jj: (0, 0, 0, 0))],
        out_specs=pl.BlockSpec((blk, width), lambda t, ii, jj: (ii[t], 0)),
        scratch_shapes=[pltpu.VMEM((n_heads, blk, 1), F32),
                        pltpu.VMEM((n_heads, blk, 1), F32),
                        pltpu.VMEM((n_heads, blk, HEAD_DIM), F32)],
    )
    return pl.pallas_call(
        kern,
        grid_spec=grid_spec,
        out_shape=jax.ShapeDtypeStruct((s, width), BF16),
        compiler_params=_params("arbitrary"),
        name="prompt_attention",
    )(jnp.asarray(ii), jnp.asarray(jj), q, k, v, mask, bias_tiles)


def _sidx_kernel(pt_ref, iq_ref, iw_ref, cache_ref, o_ref, buf_ref, sem_ref, *, chunk_pages):
    b = pl.program_id(0)
    nb = pl.num_programs(0)
    n_pages, page = buf_ref.shape[1], buf_ref.shape[2]

    def page_copy(bb, slot, p):
        return pltpu.make_async_copy(cache_ref.at[0, pt_ref[bb, p]], buf_ref.at[slot, p], sem_ref.at[slot])

    def start_all(bb, slot):
        def body(p, carry):
            page_copy(bb, slot, p).start()
            return carry
        lax.fori_loop(0, n_pages, body, 0)

    @pl.when(b == 0)
    def _():
        start_all(0, 0)

    @pl.when(b + 1 < nb)
    def _():
        start_all(b + 1, (b + 1) % 2)

    slot = b % 2

    def wait_body(p, carry):
        page_copy(b, slot, p).wait()
        return carry
    lax.fori_loop(0, n_pages, wait_body, 0)

    iq = iq_ref[0]
    w = iw_ref[0]
    ck = chunk_pages * page

    def chunk(c, carry):
        p0 = pl.multiple_of(c * chunk_pages, chunk_pages)
        kb = buf_ref[slot, pl.ds(p0, chunk_pages)].reshape(ck, IDX_DIM).astype(BF16)
        d = lax.dot_general(iq, kb, (((1,), (1,)), ((), ())), preferred_element_type=F32)
        sc = jnp.sum(jnp.maximum(d, 0.0) * w, axis=0, keepdims=True) + 0.0
        o_ref[0, :, pl.ds(pl.multiple_of(c * ck, ck), ck)] = sc
        return carry
    lax.fori_loop(0, n_pages // chunk_pages, chunk, 0)


def _sample_scores(iq, iw, cache_idx_k, page_table):
    nb, n_pages = page_table.shape
    page = cache_idx_k.shape[2]
    chunk_pages = _tile(n_pages, 8)
    kern = functools.partial(_sidx_kernel, chunk_pages=chunk_pages)
    grid_spec = pltpu.PrefetchScalarGridSpec(
        num_scalar_prefetch=1,
        grid=(nb,),
        in_specs=[pl.BlockSpec((1, N_IDX_HEADS, IDX_DIM), lambda b, pt: (b, 0, 0)),
                  pl.BlockSpec((1, N_IDX_HEADS, 1), lambda b, pt: (b, 0, 0)),
                  pl.BlockSpec(memory_space=pl.ANY)],
        out_specs=pl.BlockSpec((1, 1, n_pages * page), lambda b, pt: (b, 0, 0)),
        scratch_shapes=[pltpu.VMEM((2, n_pages, page, IDX_DIM), F32),
                        pltpu.SemaphoreType.DMA((2,))],
    )
    return pl.pallas_call(
        kern,
        grid_spec=grid_spec,
        out_shape=jax.ShapeDtypeStruct((nb, 1, n_pages * page), F32),
        compiler_params=_params("arbitrary"),
        name="sample_scores",
    )(page_table, iq, iw, cache_idx_k)


def _sthr_kernel(sc_ref, iq_ref, iw_ref, ikn_ref, idx_ref, nkeep_ref, seln_ref, key_ref, slot_ref,
                 *, n_sel, n_slot, tk):
    nb, past = sc_ref.shape
    nc = past // tk
    nlane = tk // LANES

    prod = iq_ref[...].astype(F32) * ikn_ref[...].astype(F32)
    dn = jnp.sum(prod, axis=2)
    sn = jnp.sum(jnp.maximum(dn, 0.0) * iw_ref[...], axis=1, keepdims=True) + 0.0
    kn = _sort_key(jnp.broadcast_to(sn, (nb, LANES)))

    def to_keys(c, carry):
        k0 = pl.multiple_of(c * tk, tk)
        key_ref[:, pl.ds(k0, tk)] = _sort_key(sc_ref[:, pl.ds(k0, tk)])
        return carry
    lax.fori_loop(0, nc, to_keys, 0)

    def count_ge(cand):
        def body(c, cnt):
            k0 = pl.multiple_of(c * tk, tk)
            kk = key_ref[:, pl.ds(k0, tk)]
            for j in range(nlane):
                cnt = cnt + jnp.where(kk[:, j * LANES:(j + 1) * LANES] >= cand, 1, 0)
            return cnt
        cnt = lax.fori_loop(0, nc, body, jnp.zeros((nb, LANES), I32))
        return (jnp.sum(cnt.astype(F32), axis=1, keepdims=True)
                + jnp.where(kn[:, :1] >= cand[:, :1], 1.0, 0.0))

    thr = _bisect_threshold(count_ge, nb, n_sel)
    cgt = count_ge(thr + 1)
    need = n_sel - cgt
    upper = (lax.broadcasted_iota(I32, (tk, tk), 0)
             < lax.broadcasted_iota(I32, (tk, tk), 1)).astype(BF16)
    slot_iota = lax.broadcasted_iota(I32, (n_slot, tk), 0).astype(F32)
    lane_iota = lax.broadcasted_iota(I32, (n_slot, tk), 1).astype(F32)
    slot_ref[...] = jnp.zeros(slot_ref.shape, F32)

    def body(c, carry):
        n_eq, n_kept = carry
        k0 = pl.multiple_of(c * tk, tk)
        kk = key_ref[:, pl.ds(k0, tk)]
        thr_t = jnp.concatenate([thr] * nlane, axis=1)
        eq = kk == thr_t
        eqf = jnp.where(eq, 1.0, 0.0)
        before = jnp.dot(eqf.astype(BF16), upper, preferred_element_type=F32) + n_eq
        keep = jnp.logical_or(kk > thr_t, jnp.logical_and(eq, before < need))
        keepf = jnp.where(keep, 1.0, 0.0)
        rank = jnp.dot(keepf.astype(BF16), upper, preferred_element_type=F32) + n_kept
        rank = jnp.where(keep, rank, -1.0)
        pos = lane_iota + k0.astype(F32)
        for b in range(nb):
            hit = jnp.broadcast_to(rank[b:b + 1, :], (n_slot, tk)) == slot_iota
            slot_ref[b] += jnp.where(hit, pos, 0.0)
        return (n_eq + jnp.sum(eqf, axis=1, keepdims=True),
                n_kept + jnp.sum(keepf, axis=1, keepdims=True))
    zero = jnp.zeros((nb, 1), F32)
    n_eq, n_kept = lax.fori_loop(0, nc, body, (zero, zero))
    for b in range(nb):
        idx_ref[b] = jnp.sum(slot_ref[b], axis=1, keepdims=True).astype(I32)
    nkeep_ref[...] = jnp.broadcast_to(n_kept, (nb, LANES))
    keep_new = jnp.logical_or(kn > thr, jnp.logical_and(kn == thr, n_eq < need))
    seln_ref[...] = jnp.where(keep_new, 1.0, 0.0)


def _sample_select(scores, iq, iw, ik_new, n_sel, n_slot):
    nb, past = scores.shape
    tk = _tile(past, 256)
    kern = functools.partial(_sthr_kernel, n_sel=n_sel, n_slot=n_slot, tk=tk)
    return pl.pallas_call(
        kern,
        out_shape=[jax.ShapeDtypeStruct((nb, n_slot, 1), I32),
                   jax.ShapeDtypeStruct((nb, LANES), F32),
                   jax.ShapeDtypeStruct((nb, LANES), F32)],
        scratch_shapes=[pltpu.VMEM((nb, past), I32),
                        pltpu.VMEM((nb, n_slot, tk), F32)],
        compiler_params=_params(),
        name="sample_select",
    )(scores, iq, iw, ik_new)


def _sattn_kernel(pt_ref, idx_ref, q_ref, ck_ref, cv_ref, pos_ref, nkeep_ref, kn_ref, vn_ref, seln_ref,
                  rb_ref, o_ref, kbuf_ref, vbuf_ref, sem_ref, *, n_heads, n_slot, page, past, rows):
    b = pl.program_id(0)
    nb = pl.num_programs(0)
    scale = HEAD_DIM ** -0.5

    def row_copies(bb, slot, r):
        s = idx_ref[bb, r]
        phys = pt_ref[bb, s // page]
        off = s % page
        dst = pl.ds(r * n_heads, n_heads)
        return (pltpu.make_async_copy(ck_ref.at[0, phys, off], kbuf_ref.at[slot, dst], sem_ref.at[0, slot]),
                pltpu.make_async_copy(cv_ref.at[0, phys, off], vbuf_ref.at[slot, dst], sem_ref.at[1, slot]))

    def start_all(bb, slot):
        def body(r, carry):
            ck, cv = row_copies(bb, slot, r)
            ck.start()
            cv.start()
            return carry
        lax.fori_loop(0, n_slot, body, 0)

    @pl.when(b == 0)
    def _():
        start_all(0, 0)

    @pl.when(b + 1 < nb)
    def _():
        start_all(b + 1, (b + 1) % 2)

    slot = b % 2

    def wait_body(r, carry):
        ck, cv = row_copies(b, slot, r)
        ck.wait()
        cv.wait()
        return carry
    lax.fori_loop(0, n_slot, wait_body, 0)

    pos = pos_ref[0]
    bucket = _t5_bucket(past - pos)
    valid = lax.broadcasted_iota(I32, (1, n_slot), 1).astype(F32) < nkeep_ref[0, :, 0:1]
    vnew = seln_ref[0, :, 0:1] > 0.0
    for h in range(n_heads):
        sl = slice(h * HEAD_DIM, (h + 1) * HEAD_DIM)
        bias = jnp.zeros((1, n_slot), F32)
        for bk in range(N_BUCKETS):
            bias = jnp.where(bucket == bk, rb_ref[bk, h], bias)
        qh = q_ref[0, :, sl]
        kh = kbuf_ref[slot, pl.ds(h, n_slot, stride=n_heads), :].astype(BF16)
        vh = vbuf_ref[slot, pl.ds(h, n_slot, stride=n_heads), :].astype(BF16)
        s = lax.dot_general(jnp.broadcast_to(qh, (rows, HEAD_DIM)), kh, (((1,), (1,)), ((), ())),
                            preferred_element_type=F32) * scale
        s = jnp.where(valid, s + bias, -jnp.inf)
        knh = kn_ref[0, :, sl].astype(BF16).astype(F32)
        vnh = vn_ref[0, :, sl].astype(BF16).astype(F32)
        sn = jnp.sum(qh.astype(F32) * knh, axis=1, keepdims=True) * scale + rb_ref[0, h]
        sn = jnp.where(vnew, sn, -jnp.inf)
        m = jnp.maximum(jnp.max(s, axis=1, keepdims=True), sn)
        pr = jnp.exp(s - m)
        pn = jnp.exp(sn - m)
        l = jnp.sum(pr, axis=1, keepdims=True) + pn
        acc = (jnp.dot(pr.astype(BF16), vh, preferred_element_type=F32)
               + pn.astype(BF16).astype(F32) * vnh)
        o_ref[0, :, sl] = (acc[0:1] / l[0:1]).astype(o_ref.dtype)


def _sample_attention(q, k_new, v_new, cache_k, cache_v, idx, nkeep, sel_new, page_table, rel_bias):
    nb, n_pages = page_table.shape
    _, n_phys, page, n_heads, _ = cache_k.shape
    width = n_heads * HEAD_DIM
    n_slot = idx.shape[1]
    rows = 16
    kern = functools.partial(_sattn_kernel, n_heads=n_heads, n_slot=n_slot, page=page,
                             past=n_pages * page, rows=rows)
    per_b = lambda b, pt, ix: (b, 0, 0)
    grid_spec = pltpu.PrefetchScalarGridSpec(
        num_scalar_prefetch=2,
        grid=(nb,),
        in_specs=[pl.BlockSpec((1, 1, width), per_b),
                  pl.BlockSpec(memory_space=pl.ANY),
                  pl.BlockSpec(memory_space=pl.ANY),
                  pl.BlockSpec((1, 1, n_slot), per_b),
                  pl.BlockSpec((1, 1, LANES), per_b),
                  pl.BlockSpec((1, 1, width), per_b),
                  pl.BlockSpec((1, 1, width), per_b),
                  pl.BlockSpec((1, 1, LANES), per_b),
                  pl.BlockSpec(memory_space=pltpu.SMEM)],
        out_specs=pl.BlockSpec((1, 1, width), per_b),
        scratch_shapes=[pltpu.VMEM((2, n_slot * n_heads, HEAD_DIM), F32),
                        pltpu.VMEM((2, n_slot * n_heads, HEAD_DIM), F32),
                        pltpu.SemaphoreType.DMA((2, 2))],
    )
    return pl.pallas_call(
        kern,
        grid_spec=grid_spec,
        out_shape=jax.ShapeDtypeStruct((nb, 1, width), BF16),
        compiler_params=_params("arbitrary"),
        name="sample_attention",
    )(page_table, idx, q, cache_k, cache_v, idx.reshape(nb, 1, n_slot), nkeep.reshape(nb, 1, LANES),
      k_new, v_new, sel_new.reshape(nb, 1, LANES), rel_bias)


def _pool_kernel(u_ref, prev_ref, w_ref, ps_ref, o_ref, *, tm, pos0):
    i = pl.program_id(1)
    cur = u_ref[0]
    prev = jnp.where(i == 0, 0.0, prev_ref[0])
    ext = jnp.concatenate([prev, cur], axis=0)
    gdim = cur.shape[1] // len(POOL_WINDOWS)
    t = pos0 + i * tm + lax.broadcasted_iota(I32, (tm, 1), 0)
    for g, win in enumerate(POOL_WINDOWS):
        cs = slice(g * gdim, (g + 1) * gdim)
        s = ext[:, cs]
        span = 1
        while span < win:
            s = s[span:] + s[:-span]
            span *= 2
        wsum = s[s.shape[0] - tm:]
        cnt = jnp.minimum(t + 1, win).astype(F32)
        pooled = (wsum / cnt - cur[:, cs]).astype(BF16)
        mixed = jnp.dot(pooled, w_ref[g], preferred_element_type=F32)
        o_ref[0, :, cs] = (mixed * ps_ref[:, cs]).astype(o_ref.dtype)


def _pool(u, w_pool, pool_scale, pos0):
    nb, n, width = u.shape
    halo = POOL_CTX + 1
    tm = _tile(n, 512)
    ratio = tm // halo
    kern = functools.partial(_pool_kernel, tm=tm, pos0=pos0)
    return pl.pallas_call(
        kern,
        grid=(nb, n // tm),
        in_specs=[pl.BlockSpec((1, tm, width), lambda b, i: (b, i, 0)),
                  pl.BlockSpec((1, halo, width), lambda b, i: (b, jnp.maximum(i * ratio - 1, 0), 0)),
                  pl.BlockSpec(w_pool.shape, lambda b, i: (0, 0, 0)),
                  pl.BlockSpec((1, width), lambda b, i: (0, 0))],
        out_specs=pl.BlockSpec((1, tm, width), lambda b, i: (b, i, 0)),
        out_shape=jax.ShapeDtypeStruct((nb, n, width), BF16),
        compiler_params=_params("arbitrary", "arbitrary"),
        name="pool",
    )(u, u, w_pool, pool_scale)


def _gateproj_kernel(a_ref, p_ref, wa_ref, wp_ref, ga_ref, gb_ref, o_ref):
    a = jnp.dot(a_ref[...], wa_ref[...], preferred_element_type=F32)
    p = jnp.dot(p_ref[...], wp_ref[...], preferred_element_type=F32)
    o_ref[...] = (_sigmoid(ga_ref[...]) * a + _sigmoid(gb_ref[...]) * p).astype(o_ref.dtype)


def _gateproj(attn, pooled, wa, wp, ga, gb):
    m, ka = attn.shape
    kp = pooled.shape[1]
    n = wa.shape[1]
    tm = _tile(m, 512)
    tn = _tile(n, 512)
    return pl.pallas_call(
        _gateproj_kernel,
        grid=(m // tm, n // tn),
        in_specs=[pl.BlockSpec((tm, ka), lambda i, j: (i, 0)),
                  pl.BlockSpec((tm, kp), lambda i, j: (i, 0)),
                  pl.BlockSpec((ka, tn), lambda i, j: (0, j)),
                  pl.BlockSpec((kp, tn), lambda i, j: (0, j)),
                  pl.BlockSpec((tm, tn), lambda i, j: (i, j)),
                  pl.BlockSpec((tm, tn), lambda i, j: (i, j))],
        out_specs=pl.BlockSpec((tm, tn), lambda i, j: (i, j)),
        out_shape=jax.ShapeDtypeStruct((m, n), BF16),
        compiler_params=_params("arbitrary", "arbitrary"),
        name="gateproj",
    )(attn, pooled, wa, wp, ga, gb)


def _mixout_kernel(m_ref, w_ref, x_ref, gate_ref, gpost_ref, gpre_ref, sc_ref, sh_ref, x1_ref, h2_ref):
    y = jnp.dot(m_ref[...], w_ref[...], preferred_element_type=F32)
    x1 = x_ref[...] + gate_ref[...] * _rms(y, gpost_ref[...])
    x1_ref[...] = x1
    h2_ref[...] = (_rms(x1, gpre_ref[...]) * (1 + sc_ref[...]) + sh_ref[...]).astype(h2_ref.dtype)


def _mixout(m, w_out, x, gate1, g_post, g_pre_ffn, scale2, shift2):
    t, d = x.shape
    tm = _tile(t, 256)
    row = pl.BlockSpec((1, d), lambda i: (0, 0))
    tok = pl.BlockSpec((tm, d), lambda i: (i, 0))
    return pl.pallas_call(
        _mixout_kernel,
        grid=(t // tm,),
        in_specs=[tok, pl.BlockSpec((d, d), lambda i: (0, 0)), tok, _mod_spec(gate1, tm, d),
                  row, row, _mod_spec(scale2, tm, d), _mod_spec(shift2, tm, d)],
        out_specs=[tok, tok],
        out_shape=[jax.ShapeDtypeStruct((t, d), F32), jax.ShapeDtypeStruct((t, d), BF16)],
        compiler_params=_params("arbitrary"),
        name="mixout",
    )(m, w_out, x, gate1, g_post, g_pre_ffn, scale2, shift2)


def _route(h_ref, wt_ref, b_ref, tm):
    n_exp = wt_ref.shape[0]
    per_group = n_exp // N_GROUP
    logits = lax.dot_general(wt_ref[...], h_ref[...], (((1,), (1,)), ((), ())),
                             preferred_element_type=F32)
    s = _sigmoid(logits)
    sb = s + b_ref[...]
    gidx = lax.broadcasted_iota(I32, (per_group, tm), 0)
    gscores = []
    for g in range(N_GROUP):
        x = sb[g * per_group:(g + 1) * per_group]
        m1 = jnp.max(x, axis=0, keepdims=True)
        i1 = jnp.min(jnp.where(x == m1, gidx, per_group), axis=0, keepdims=True)
        m2 = jnp.max(jnp.where(gidx == i1, -jnp.inf, x), axis=0, keepdims=True)
        gscores.append(m1 + m2)
    cur = jnp.concatenate(gscores, axis=0)
    ridx = lax.broadcasted_iota(I32, (N_GROUP, tm), 0)
    gsel = jnp.zeros((N_GROUP, tm), F32)
    for _ in range(TOPK_GROUP):
        m = jnp.max(cur, axis=0, keepdims=True)
        im = jnp.min(jnp.where(cur == m, ridx, N_GROUP), axis=0, keepdims=True)
        pick = ridx == im
        gsel = jnp.where(pick, 1.0, gsel)
        cur = jnp.where(pick, -jnp.inf, cur)
    emask = jnp.concatenate(
        [jnp.broadcast_to(gsel[g:g + 1], (per_group, tm)) for g in range(N_GROUP)], axis=0)
    cur = jnp.where(emask > 0.0, sb, -jnp.inf)
    eidx = lax.broadcasted_iota(I32, (n_exp, tm), 0)
    esel = jnp.zeros((n_exp, tm), F32)
    picks = []
    for _ in range(TOP_K):
        m = jnp.max(cur, axis=0, keepdims=True)
        im = jnp.min(jnp.where(cur == m, eidx, n_exp), axis=0, keepdims=True)
        pick = eidx == im
        esel = jnp.where(pick, 1.0, esel)
        cur = jnp.where(pick, -jnp.inf, cur)
        picks.append(im)
    w = jnp.where(esel > 0.0, s, 0.0)
    gates = w / jnp.sum(w, axis=0, keepdims=True) * ROUTED_SCALE
    return gates, esel, picks


def _router_kernel(h_ref, wt_ref, b_ref, o_ref):
    gates, _, _ = _route(h_ref, wt_ref, b_ref, o_ref.shape[1])
    o_ref[...] = gates


def _router(h, w_router_t, b_router):
    t, d = h.shape
    n_exp = w_router_t.shape[0]
    tm = _tile(t, 512)
    return pl.pallas_call(
        _router_kernel,
        grid=(t // tm,),
        in_specs=[pl.BlockSpec((tm, d), lambda i: (i, 0)),
                  pl.BlockSpec((n_exp, d), lambda i: (0, 0)),
                  pl.BlockSpec((n_exp, 1), lambda i: (0, 0))],
        out_specs=pl.BlockSpec((n_exp, tm), lambda i: (0, i)),
        out_shape=jax.ShapeDtypeStruct((n_exp, t), F32),
        compiler_params=_params("arbitrary"),
        name="router",
    )(h, w_router_t, b_router)


def _router_dispatch_kernel(h_ref, wt_ref, b_ref, eid_ref, pos_ref, wt_out_ref, cnt_ref):
    tm = eid_ref.shape[1]
    n_exp = wt_ref.shape[0]
    gates, esel, picks = _route(h_ref, wt_ref, b_ref, tm)

    @pl.when(pl.program_id(0) == 0)
    def _():
        cnt_ref[...] = jnp.zeros(cnt_ref.shape, F32)

    upper = (lax.broadcasted_iota(I32, (tm, tm), 0)
             < lax.broadcasted_iota(I32, (tm, tm), 1)).astype(BF16)
    before = cnt_ref[:, 0:1] + jnp.dot(esel.astype(BF16), upper, preferred_element_type=F32)
    eidx = lax.broadcasted_iota(I32, (n_exp, tm), 0)
    for k, im in enumerate(picks):
        hit = eidx == im
        eid_ref[k:k + 1, :] = im
        pos_ref[k:k + 1, :] = jnp.sum(jnp.where(hit, before, 0.0), axis=0, keepdims=True).astype(I32)
        wt_out_ref[k:k + 1, :] = jnp.sum(jnp.where(hit, gates, 0.0), axis=0, keepdims=True)
    cnt_ref[...] = cnt_ref[...] + jnp.sum(esel, axis=1, keepdims=True)


def _router_dispatch(h, w_router_t, b_router):
    t, d = h.shape
    n_exp = w_router_t.shape[0]
    tm = _tile(t, 512)
    pick_spec = pl.BlockSpec((TOP_K, tm), lambda i: (0, i))
    return pl.pallas_call(
        _router_dispatch_kernel,
        grid=(t // tm,),
        in_specs=[pl.BlockSpec((tm, d), lambda i: (i, 0)),
                  pl.BlockSpec((n_exp, d), lambda i: (0, 0)),
                  pl.BlockSpec((n_exp, 1), lambda i: (0, 0))],
        out_specs=[pick_spec, pick_spec, pick_spec, pl.BlockSpec((n_exp, LANES), lambda i: (0, 0))],
        out_shape=[jax.ShapeDtypeStruct((TOP_K, t), I32), jax.ShapeDtypeStruct((TOP_K, t), I32),
                   jax.ShapeDtypeStruct((TOP_K, t), F32), jax.ShapeDtypeStruct((n_exp, LANES), F32)],
        compiler_params=_params("arbitrary"),
        name="router_dispatch",
    )(h, w_router_t, b_router)


def _moe_kernel(h_ref, g_ref, wg_ref, wu_ref, wd_ref, init_ref, x_ref, gate_ref, gpost_ref,
                o_ref, acc_ref, *, final):
    e = pl.program_id(1)
    n_exp = pl.num_programs(1)

    @pl.when(e == 0)
    def _():
        acc_ref[...] = init_ref[...]

    gates = g_ref[...]
    lane = lax.broadcasted_iota(I32, gates.shape, 1)
    gcol = jnp.sum(jnp.where(lane == e, gates, 0.0), axis=1, keepdims=True)

    @pl.when(jnp.max(jnp.abs(gcol)) > 0.0)
    def _():
        hh = h_ref[...]
        a = jnp.dot(hh, wg_ref[0], preferred_element_type=F32)
        b = jnp.dot(hh, wu_ref[0], preferred_element_type=F32)
        act = (a * _sigmoid(a) * b).astype(BF16)
        y = jnp.dot(act, wd_ref[0], preferred_element_type=F32)
        acc_ref[...] += y * gcol

    @pl.when(e == n_exp - 1)
    def _():
        if final:
            o_ref[...] = x_ref[...] + gate_ref[...] * _rms(acc_ref[...], gpost_ref[...])
        else:
            o_ref[...] = acc_ref[...]


def _moe(h, gates, wg, wu, wd, init, x, gate2, g_post, final):
    t, d = h.shape
    n_exp, _, f = wg.shape
    tm = _tile(t, 512)
    tok = pl.BlockSpec((tm, d), lambda i, e: (i, 0))
    kern = functools.partial(_moe_kernel, final=final)
    return pl.pallas_call(
        kern,
        grid=(t // tm, n_exp),
        in_specs=[tok,
                  pl.BlockSpec((tm, gates.shape[1]), lambda i, e: (i, 0)),
                  pl.BlockSpec((1, d, f), lambda i, e: (e, 0, 0)),
                  pl.BlockSpec((1, d, f), lambda i, e: (e, 0, 0)),
                  pl.BlockSpec((1, f, d), lambda i, e: (e, 0, 0)),
                  tok, tok, _mod_spec(gate2, tm, d),
                  pl.BlockSpec((1, d), lambda i, e: (0, 0))],
        out_specs=tok,
        out_shape=jax.ShapeDtypeStruct((t, d), F32),
        scratch_shapes=[pltpu.VMEM((tm, d), F32)],
        compiler_params=_params("arbitrary", "arbitrary"),
        name="moe_final" if final else "moe_shared",
    )(h, gates, wg, wu, wd, init, x, gate2, g_post)


def _slots_kernel(pstart_ref, eid_ref, pos_ref, o_ref):
    eid = eid_ref[...]
    base = jnp.zeros(eid.shape, I32)
    for e in range(pstart_ref.shape[0]):
        base = jnp.where(eid == e, pstart_ref[e], base)
    o_ref[...] = base + pos_ref[...]


def _slots(pstart, eid, pos):
    return pl.pallas_call(
        _slots_kernel,
        in_specs=[pl.BlockSpec(memory_space=pltpu.SMEM),
                  pl.BlockSpec(memory_space=pltpu.VMEM), pl.BlockSpec(memory_space=pltpu.VMEM)],
        out_shape=jax.ShapeDtypeStruct(eid.shape, I32),
        name="slots",
    )(pstart, eid, pos)


def _dispatch_kernel(slot_hbm, h_hbm, xs_in, xs_out, slot_smem, sem_ref, *, tm, rt):
    del xs_in
    i = pl.program_id(0)
    t0 = pl.multiple_of(i * tm, tm)
    idx_copy = pltpu.make_async_copy(slot_hbm.at[:, pl.ds(t0, tm)], slot_smem, sem_ref.at[0])
    idx_copy.start()
    idx_copy.wait()

    def row_copy(k, t):
        src = h_hbm.at[pl.ds((t0 + t) * rt, rt)]
        dst = xs_out.at[pl.ds(slot_smem[k, t] * rt, rt)]
        return pltpu.make_async_copy(src, dst, sem_ref.at[1])

    def start(t, carry):
        for k in range(TOP_K):
            row_copy(k, t).start()
        return carry
    lax.fori_loop(0, tm, start, 0)

    def wait(t, carry):
        for k in range(TOP_K):
            row_copy(k, t).wait()
        return carry
    lax.fori_loop(0, tm, wait, 0)


def _dispatch(slot, h_rows, n_slots):
    t = slot.shape[1]
    rt = h_rows.shape[0] // t
    tm = _tile(t, 256)
    xs0 = jnp.zeros((n_slots * rt, LANES), F32)
    kern = functools.partial(_dispatch_kernel, tm=tm, rt=rt)
    return pl.pallas_call(
        kern,
        grid=(t // tm,),
        in_specs=[pl.BlockSpec(memory_space=pl.ANY), pl.BlockSpec(memory_space=pl.ANY),
                  pl.BlockSpec(memory_space=pl.ANY)],
        out_specs=pl.BlockSpec(memory_space=pl.ANY),
        out_shape=jax.ShapeDtypeStruct(xs0.shape, F32),
        scratch_shapes=[pltpu.SMEM((TOP_K, tm), I32), pltpu.SemaphoreType.DMA((2,))],
        input_output_aliases={2: 0},
        compiler_params=_params("arbitrary"),
        name="dispatch",
    )(slot, h_rows, xs0)


def _expert_kernel(te_ref, nu_ref, x_ref, wg_ref, wu_ref, wd_ref, o_ref, *, ts, rt):
    s = pl.program_id(0)

    @pl.when(s < nu_ref[0])
    def _():
        x = jnp.concatenate([x_ref[pl.ds(j, ts, stride=rt), :] for j in range(rt)],
                            axis=1).astype(BF16)
        a = jnp.dot(x, wg_ref[0], preferred_element_type=F32)
        b = jnp.dot(x, wu_ref[0], preferred_element_type=F32)
        act = (a * _sigmoid(a) * b).astype(BF16)
        y = jnp.dot(act, wd_ref[0], preferred_element_type=F32)
        for j in range(rt):
            o_ref[pl.ds(j, ts, stride=rt), :] = y[:, j * LANES:(j + 1) * LANES]

    @pl.when(s >= nu_ref[0])
    def _():
        o_ref[...] = jnp.zeros(o_ref.shape, F32)


def _experts(tile_e, n_used, xs, wg, wu, wd, ts):
    n_exp, d, f = wg.shape
    n_tiles = tile_e.shape[0]
    rt = d // LANES
    kern = functools.partial(_expert_kernel, ts=ts, rt=rt)
    grid_spec = pltpu.PrefetchScalarGridSpec(
        num_scalar_prefetch=2,
        grid=(n_tiles,),
        in_specs=[pl.BlockSpec((ts * rt, LANES), lambda s, te, nu: (s, 0)),
                  pl.BlockSpec((1, d, f), lambda s, te, nu: (te[s], 0, 0)),
                  pl.BlockSpec((1, d, f), lambda s, te, nu: (te[s], 0, 0)),
                  pl.BlockSpec((1, f, d), lambda s, te, nu: (te[s], 0, 0))],
        out_specs=pl.BlockSpec((ts * rt, LANES), lambda s, te, nu: (s, 0)),
    )
    return pl.pallas_call(
        kern,
        grid_spec=grid_spec,
        out_shape=jax.ShapeDtypeStruct(xs.shape, F32),
        compiler_params=_params("arbitrary"),
        name="experts",
    )(tile_e, n_used, xs, wg, wu, wd)


def _combine_kernel(slot_hbm, ys_hbm, wt_ref, init_ref, x_ref, gate_ref, gpost_ref, o_ref,
                    slot_smem, buf_ref, sem_ref, *, tm, rt):
    i = pl.program_id(0)
    t0 = pl.multiple_of(i * tm, tm)
    idx_copy = pltpu.make_async_copy(slot_hbm.at[:, pl.ds(t0, tm)], slot_smem, sem_ref.at[0])
    idx_copy.start()
    idx_copy.wait()

    def row_copy(k, t):
        src = ys_hbm.at[pl.ds(slot_smem[k, t] * rt, rt)]
        dst = buf_ref.at[pl.ds((k * tm + t) * rt, rt)]
        return pltpu.make_async_copy(src, dst, sem_ref.at[1])

    def start(t, carry):
        for k in range(TOP_K):
            row_copy(k, t).start()
        return carry
    lax.fori_loop(0, tm, start, 0)

    def wait(t, carry):
        for k in range(TOP_K):
            row_copy(k, t).wait()
        return carry
    lax.fori_loop(0, tm, wait, 0)

    wts = wt_ref[...]
    cols = []
    for j in range(rt):
        acc = init_ref[:, j * LANES:(j + 1) * LANES]
        for k in range(TOP_K):
            part = buf_ref[pl.ds(k * tm * rt + j, tm, stride=rt), :]
            acc = acc + part * wts[:, k:k + 1]
        cols.append(acc)
    y = jnp.concatenate(cols, axis=1)
    o_ref[...] = x_ref[...] + gate_ref[...] * _rms(y, gpost_ref[...])


def _combine(slot, ys, wts, init, x, gate2, g_post):
    t, d = x.shape
    tm = _tile(t, 128)
    tok = pl.BlockSpec((tm, d), lambda i: (i, 0))
    rt = d // LANES
    kern = functools.partial(_combine_kernel, tm=tm, rt=rt)
    return pl.pallas_call(
        kern,
        grid=(t // tm,),
        in_specs=[pl.BlockSpec(memory_space=pl.ANY), pl.BlockSpec(memory_space=pl.ANY),
                  pl.BlockSpec((tm, TOP_K), lambda i: (i, 0)), tok, tok, _mod_spec(gate2, tm, d),
                  pl.BlockSpec((1, d), lambda i: (0, 0))],
        out_specs=tok,
        out_shape=jax.ShapeDtypeStruct((t, d), F32),
        scratch_shapes=[pltpu.SMEM((TOP_K, tm), I32),
                        pltpu.VMEM((TOP_K * tm * rt, LANES), F32),
                        pltpu.SemaphoreType.DMA((2,))],
        compiler_params=_params("arbitrary"),
        name="combine",
    )(slot, ys, wts, init, x, gate2, g_post)


def _routed_by_dispatch(h2, x1, shared, g2, wts):
    t, d = h2.shape
    rt = d // LANES
    n_exp = wts["w_router_t"].shape[0]
    ts = 256
    eid, pos, pick_w, counts = _router_dispatch(h2, wts["w_router_t"], wts["b_router"])
    cnt = counts[:, 0].astype(I32)
    padded = (cnt + ts - 1) // ts * ts
    pend = jnp.cumsum(padded)
    pstart = pend - padded
    n_tiles = (t * TOP_K) // ts + n_exp
    tile_e = jnp.minimum(jnp.searchsorted(pend, jnp.arange(n_tiles, dtype=I32) * ts, side="right"),
                         n_exp - 1).astype(I32)
    n_used = (pend[-1:] // ts).astype(I32)
    slot = _slots(pstart, eid, pos)
    h_rows = h2.astype(F32).reshape(t * rt, LANES)
    xs = _dispatch(slot, h_rows, n_tiles * ts)
    ys = _experts(tile_e, n_used, xs, wts["w_gate_e"], wts["w_up_e"], wts["w_down_e"], ts)
    return _combine(slot, ys, pick_w.T, shared, x1, g2, wts["g_post_ffn"])


def _token_path(x, mods, wts):
    sh1, sc1 = mods[0], mods[1]
    h = _normmod(x, wts["g_pre_mix"], sc1, sh1)
    q = _mm(h, wts["w_q"], BF16)
    k = _mm(h, wts["w_k"], F32)
    v = _mm(h, wts["w_v"], F32)
    iq = _mm(h, wts["w_iq"], BF16)
    u = _mm(h, wts["w_u"], F32)
    ga = _mm(h, wts["w_ga"], F32)
    gb = _mm(h, wts["w_gb"], F32)
    ik, iw = _idxproj(h, wts["w_idx"], wts["idx_k_norm"])
    return q, k, v, iq, iw, ik, u, ga, gb


def _back_half(x, attn, pooled, ga, gb, mods, wts, dispatch):
    _, _, g1, sh2, sc2, g2 = mods
    m = _gateproj(attn, pooled, wts["w_proj_attn"], wts["w_proj_pool"], ga, gb)
    x1, h2 = _mixout(m, wts["w_out"], x, g1, wts["g_post_mix"], wts["g_pre_ffn"], sc2, sh2)
    t, d = x.shape
    ones = jnp.ones((t, 1), F32)
    zeros = jnp.zeros((t, d), F32)
    shared = _moe(h2, ones, wts["w_gate_s"], wts["w_up_s"], wts["w_down_s"], zeros, x1, g2,
                  wts["g_post_ffn"], final=False)
    if dispatch:
        return _routed_by_dispatch(h2, x1, shared, g2, wts)
    gates = _router(h2, wts["w_router_t"], wts["b_router"]).T
    return _moe(h2, gates, wts["w_gate_e"], wts["w_up_e"], wts["w_down_e"], shared, x1, g2,
                wts["g_post_ffn"], final=True)


def kernel(x_prompt, x_sample, c_prompt, c_sample, cache_k, cache_v, cache_idx_k, state_pool, page_table, w_ada, b_ada, g_pre_mix, w_in, idx_k_norm, rel_bias, w_pool, pool_scale, w_proj_attn, w_proj_pool, w_out, g_post_mix, g_pre_ffn, w_router, b_router, w_gate_e, w_up_e, w_down_e, w_gate_s, w_up_s, w_down_s, g_post_ffn):
    depth = w_ada.shape[0]
    assert depth == 1, "single-layer trunk"
    nbp, seq, d = x_prompt.shape
    assert nbp == 1, "one prompt sequence"
    nbs, dec_seq, _ = x_sample.shape
    assert dec_seq == 1, "one new token per sample sequence"
    _, n_phys, page, n_heads, head_dim = cache_k.shape
    assert head_dim == HEAD_DIM and page == LANES
    aw = n_heads * HEAD_DIM
    pw = state_pool.shape[-1]
    n_pages = page_table.shape[1]
    past = n_pages * page
    iqw = N_IDX_HEADS * IDX_DIM

    win = w_in[0]
    offs = np.cumsum([0, aw, aw, aw, iqw, IDX_DIM, N_IDX_HEADS, pw, d, d])
    seg = lambda a: win[:, offs[a]:offs[a + 1]].astype(BF16)
    w_idx = jnp.pad(win[:, offs[4]:offs[6]], ((0, 0), (0, LANES - IDX_DIM - N_IDX_HEADS))).astype(BF16)
    wts = dict(
        g_pre_mix=g_pre_mix, idx_k_norm=idx_k_norm,
        w_q=seg(0), w_k=seg(1), w_v=seg(2), w_iq=seg(3), w_idx=w_idx, w_u=seg(6), w_ga=seg(7), w_gb=seg(8),
        w_proj_attn=w_proj_attn[0].astype(BF16), w_proj_pool=w_proj_pool[0].astype(BF16),
        w_out=w_out[0].astype(BF16), g_post_mix=g_post_mix, g_pre_ffn=g_pre_ffn,
        w_router_t=w_router[0].T.astype(BF16), b_router=b_router.reshape(-1, 1),
        w_gate_e=w_gate_e[0].astype(BF16), w_up_e=w_up_e[0].astype(BF16), w_down_e=w_down_e[0].astype(BF16),
        w_gate_s=w_gate_s.astype(BF16), w_up_s=w_up_s.astype(BF16), w_down_s=w_down_s.astype(BF16),
        g_post_ffn=g_post_ffn,
    )
    w_pool_b = w_pool[0].astype(BF16)

    n_c = nbp + nbs
    rows = -(-n_c // SUBLANES) * SUBLANES
    c_all = jnp.pad(jnp.concatenate([c_prompt, c_sample], axis=0), ((0, rows - n_c), (0, 0)))
    mod = _ada(c_all, w_ada[0], b_ada)
    mods_p = [mod[0:1, a * d:(a + 1) * d] for a in range(6)]
    mods_s = [mod[nbp:n_c, a * d:(a + 1) * d] for a in range(6)]

    bias_tiles = _bias_tiles(rel_bias)

    xp = x_prompt.reshape(seq, d)
    q, k, v, iq, iw, ik, u, ga, gb = _token_path(xp, mods_p, wts)
    mask = _prompt_select(iq, ik.astype(BF16).T, iw, min(TOPK_MAX, seq // 4))
    attn = _prompt_attention(q, k.astype(BF16), v.astype(BF16), mask, bias_tiles)
    pooled = _pool(u.reshape(1, seq, pw), w_pool_b, pool_scale, 0).reshape(seq, pw)
    y_prompt = _back_half(xp, attn, pooled, ga, gb, mods_p, wts, True).reshape(nbp, seq, d)
    k_prompt = k.reshape(1, nbp, seq, n_heads, HEAD_DIM)
    v_prompt = v.reshape(1, nbp, seq, n_heads, HEAD_DIM)
    idxk_prompt = ik.reshape(1, nbp, seq, IDX_DIM)
    pool_prompt = u[seq - POOL_CTX:].reshape(1, nbp, POOL_CTX, pw)

    xs = x_sample.reshape(nbs, d)
    qs, ks, vs, iqs, iws, iks, us, gas, gbs = _token_path(xs, mods_s, wts)
    iq3 = iqs.reshape(nbs, N_IDX_HEADS, IDX_DIM)
    iw3 = iws.reshape(nbs, N_IDX_HEADS, 1)
    scores = _sample_scores(iq3, iw3, cache_idx_k, page_table).reshape(nbs, past)
    n_sel_s = min(TOPK_MAX, (past + dec_seq) // 4)
    n_slot = -(-n_sel_s // LANES) * LANES
    idx, nkeep, sel_new = _sample_select(scores, iq3, iws, iks.astype(BF16).reshape(nbs, 1, IDX_DIM),
                                         n_sel_s, n_slot)
    attn_s = _sample_attention(
        qs.reshape(nbs, 1, aw), ks.reshape(nbs, 1, aw), vs.reshape(nbs, 1, aw), cache_k, cache_v,
        idx.reshape(nbs, n_slot), nkeep, sel_new, page_table, rel_bias).reshape(nbs, aw)
    ext = jnp.concatenate([state_pool[0], us.reshape(nbs, 1, pw)], axis=1)
    pooled_s = _pool(ext, w_pool_b, pool_scale, past - POOL_CTX)[:, POOL_CTX]
    y_sample = _back_half(xs, attn_s, pooled_s, gas, gbs, mods_s, wts, False).reshape(nbs, 1, d)
    k_sample = ks.reshape(1, nbs, 1, n_heads, HEAD_DIM)
    v_sample = vs.reshape(1, nbs, 1, n_heads, HEAD_DIM)
    idxk_sample = iks.reshape(1, nbs, 1, IDX_DIM)
    pool_sample = ext[:, 1:].reshape(1, nbs, POOL_CTX, pw)

    return (y_prompt, y_sample, k_prompt, v_prompt, idxk_prompt, pool_prompt,
            k_sample, v_sample, idxk_sample, pool_sample)
```

```python
import functools
import math

import jax
import jax.numpy as jnp
import numpy as np
from jax import lax
from jax.experimental import pallas as pl
from jax.experimental.pallas import tpu as pltpu

F32 = jnp.float32
BF16 = jnp.bfloat16
I32 = jnp.int32

HEAD_DIM = 128
N_IDX_HEADS = 16
IDX_DIM = 64
TOPK_MAX = 256
N_BUCKETS = 32
MAX_DISTANCE = 128
POOL_WINDOWS = (2, 4, 8, 16)
POOL_CTX = max(POOL_WINDOWS) - 1
N_GROUP = 8
TOPK_GROUP = 4
TOP_K = 8
ROUTED_SCALE = 2.5
EPS = 1e-6

LANES = 128
SUBLANES = 8
VMEM_LIMIT_BYTES = 56 * 1024 * 1024

NEG_BIG = -1e30
NEG_INF_KEY = int(np.int32(np.uint32(0xFF800000)) ^ np.int32(0x7FFFFFFF))
INT_MIN = -(2 ** 31)


def _params(*sem):
    return pltpu.CompilerParams(dimension_semantics=sem, vmem_limit_bytes=VMEM_LIMIT_BYTES)


def _tile(n, pref):
    if n <= pref:
        return n
    t = pref
    while n % t:
        t //= 2
    return t


def _rms(x, g):
    return x * lax.rsqrt(jnp.mean(x * x, axis=-1, keepdims=True) + EPS) * g


def _sigmoid(x):
    return jax.nn.sigmoid(x)


def _ada_kernel(c_ref, w_ref, b_ref, o_ref):
    c = c_ref[...]
    a = (c * _sigmoid(c)).astype(BF16)
    o_ref[...] = jnp.dot(a, w_ref[...].astype(BF16), preferred_element_type=F32) + b_ref[...]


def _ada(c, w, b):
    r, d = c.shape
    n = w.shape[1]
    tn = _tile(n, 1024)
    return pl.pallas_call(
        _ada_kernel,
        grid=(n // tn,),
        in_specs=[pl.BlockSpec((r, d), lambda j: (0, 0)),
                  pl.BlockSpec((d, tn), lambda j: (0, j)),
                  pl.BlockSpec((1, tn), lambda j: (0, j))],
        out_specs=pl.BlockSpec((r, tn), lambda j: (0, j)),
        out_shape=jax.ShapeDtypeStruct((r, n), F32),
        compiler_params=_params("arbitrary"),
        name="ada",
    )(c, w, b)


def _normmod_kernel(x_ref, g_ref, sc_ref, sh_ref, o_ref):
    y = _rms(x_ref[...], g_ref[...])
    o_ref[...] = (y * (1 + sc_ref[...]) + sh_ref[...]).astype(o_ref.dtype)


def _mod_spec(mod, tm, d):
    if mod.shape[0] == 1:
        return pl.BlockSpec((1, d), lambda i, *_: (0, 0))
    return pl.BlockSpec((tm, d), lambda i, *_: (i, 0))


def _normmod(x, g, scale, shift):
    t, d = x.shape
    tm = _tile(t, 512)
    return pl.pallas_call(
        _normmod_kernel,
        grid=(t // tm,),
        in_specs=[pl.BlockSpec((tm, d), lambda i: (i, 0)),
                  pl.BlockSpec((1, d), lambda i: (0, 0)),
                  _mod_spec(scale, tm, d), _mod_spec(shift, tm, d)],
        out_specs=pl.BlockSpec((tm, d), lambda i: (i, 0)),
        out_shape=jax.ShapeDtypeStruct((t, d), BF16),
        compiler_params=_params("arbitrary"),
        name="normmod",
    )(x, g, scale, shift)


def _mm_kernel(a_ref, w_ref, o_ref):
    o_ref[...] = jnp.dot(a_ref[...], w_ref[...], preferred_element_type=F32).astype(o_ref.dtype)


def _mm(a, w, out_dtype):
    m, k = a.shape
    n = w.shape[1]
    tm = _tile(m, 1024)
    tn = _tile(n, 1024)
    return pl.pallas_call(
        _mm_kernel,
        grid=(m // tm, n // tn),
        in_specs=[pl.BlockSpec((tm, k), lambda i, j: (i, 0)),
                  pl.BlockSpec((k, tn), lambda i, j: (0, j))],
        out_specs=pl.BlockSpec((tm, tn), lambda i, j: (i, j)),
        out_shape=jax.ShapeDtypeStruct((m, n), out_dtype),
        compiler_params=_params("arbitrary", "arbitrary"),
        name="mm",
    )(a, w)


def _idxproj_kernel(a_ref, w_ref, g_ref, ik_ref, iw_ref):
    z = jnp.dot(a_ref[...], w_ref[...], preferred_element_type=F32)
    ik = z[:, :IDX_DIM]
    xc = ik - jnp.mean(ik, axis=-1, keepdims=True)
    y = xc * lax.rsqrt(jnp.mean(xc * xc, axis=-1, keepdims=True) + EPS)
    ik_ref[...] = y * g_ref[...]
    iw_ref[...] = z[:, IDX_DIM:IDX_DIM + N_IDX_HEADS] * (IDX_DIM ** -0.5 * N_IDX_HEADS ** -0.5)


def _idxproj(a, w, g):
    m, k = a.shape
    tm = _tile(m, 512)
    return pl.pallas_call(
        _idxproj_kernel,
        grid=(m // tm,),
        in_specs=[pl.BlockSpec((tm, k), lambda i: (i, 0)),
                  pl.BlockSpec((k, LANES), lambda i: (0, 0)),
                  pl.BlockSpec((1, IDX_DIM), lambda i: (0, 0))],
        out_specs=[pl.BlockSpec((tm, IDX_DIM), lambda i: (i, 0)),
                   pl.BlockSpec((tm, N_IDX_HEADS), lambda i: (i, 0))],
        out_shape=[jax.ShapeDtypeStruct((m, IDX_DIM), F32),
                   jax.ShapeDtypeStruct((m, N_IDX_HEADS), F32)],
        compiler_params=_params("arbitrary"),
        name="idxproj",
    )(a, w, g)


def _t5_bucket(rel):
    n = jnp.maximum(rel, 0)
    max_exact = N_BUCKETS // 2
    nf = jnp.maximum(n, 1).astype(F32)
    large = max_exact + (jnp.log(nf / max_exact) / math.log(MAX_DISTANCE / max_exact)
                         * (N_BUCKETS - max_exact)).astype(I32)
    large = jnp.minimum(large, N_BUCKETS - 1)
    return jnp.where(n < max_exact, n, large)


def _bias_kernel(rb_ref, o_ref):
    n_heads = o_ref.shape[1]
    r = lax.broadcasted_iota(I32, (LANES, LANES), 0)
    c = lax.broadcasted_iota(I32, (LANES, LANES), 1)
    for d in range(2):
        bucket = _t5_bucket(d * LANES + r - c)
        for h in range(n_heads):
            val = jnp.zeros((LANES, LANES), F32)
            for b in range(N_BUCKETS):
                val = jnp.where(bucket == b, rb_ref[b, h], val)
            o_ref[d, h] = val - rb_ref[N_BUCKETS - 1, h]


def _bias_tiles(rel_bias):
    n_heads = rel_bias.shape[1]
    return pl.pallas_call(
        _bias_kernel,
        in_specs=[pl.BlockSpec(memory_space=pltpu.SMEM)],
        out_shape=jax.ShapeDtypeStruct((2, n_heads, LANES, LANES), F32),
        name="bias_tiles",
    )(rel_bias)


def _sort_key(x):
    bits = pltpu.bitcast(x, I32)
    return jnp.where(bits < 0, bits ^ 0x7FFFFFFF, bits)


def _bisect_threshold(count_ge, rows, n_sel):
    def body(it, prefix):
        bit = 31 - it
        cand = prefix + jnp.left_shift(jnp.int32(1), bit)
        cnt = count_ge(cand)
        return jnp.where(cnt >= n_sel, cand, prefix)

    return lax.fori_loop(0, 32, body, jnp.full((rows, LANES), INT_MIN, I32))


def _idx_kernel(iq_ref, ikt_ref, iw_ref, mask_ref, key_ref, wb_ref, *, tq, tk, ck, rg, n_sel):
    i = pl.program_id(0)
    s_total = ikt_ref.shape[1]
    q0 = i * tq
    nc = (q0 + tq) // tk
    nlane = tk // LANES

    for h in range(N_IDX_HEADS):
        wb_ref[h] = jnp.broadcast_to(iw_ref[:, h:h + 1], (tq, LANES))

    def score_chunk(c, carry):
        k0 = pl.multiple_of(c * tk, tk)
        kt = ikt_ref[:, pl.ds(k0, tk)]
        acc = jnp.zeros((tq, tk), F32)
        for h in range(N_IDX_HEADS):
            d = jnp.dot(iq_ref[:, h * IDX_DIM:(h + 1) * IDX_DIM], kt, preferred_element_type=F32)
            w = jnp.concatenate([wb_ref[h]] * nlane, axis=1)
            acc = acc + jnp.maximum(d, 0.0) * w
        acc = acc + 0.0
        t = q0 + lax.broadcasted_iota(I32, (tq, tk), 0)
        s = k0 + lax.broadcasted_iota(I32, (tq, tk), 1)
        acc = jnp.where(s <= t, acc, -jnp.inf)
        key_ref[:, pl.ds(k0, tk)] = _sort_key(acc)
        return carry

    lax.fori_loop(0, nc, score_chunk, 0)

    ncc = (nc * tk + ck - 1) // ck

    def fill_chunk(c, carry):
        key_ref[:, pl.ds(pl.multiple_of(c * tk, tk), tk)] = jnp.full((tq, tk), NEG_INF_KEY, I32)
        return carry
    lax.fori_loop(nc, ncc * (ck // tk), fill_chunk, 0)

    ones = jnp.ones((LANES, LANES), BF16)

    def count_ge(cand):
        parts = []
        for g in range(tq // rg):
            rows = pl.ds(g * rg, rg)
            cand_g = cand[g * rg:(g + 1) * rg]

            def body(c, cnt, rows=rows, cand_g=cand_g):
                kk = key_ref[rows, pl.ds(pl.multiple_of(c * ck, ck), ck)]
                for j in range(ck // LANES):
                    cnt = cnt + jnp.where(kk[:, j * LANES:(j + 1) * LANES] >= cand_g, 1, 0)
                return cnt
            parts.append(lax.fori_loop(0, ncc, body, jnp.zeros((rg, LANES), I32)))
        cnt = jnp.concatenate(parts, axis=0) if len(parts) > 1 else parts[0]
        return jnp.dot(cnt.astype(F32).astype(BF16), ones, preferred_element_type=F32)

    thr = _bisect_threshold(count_ge, tq, n_sel)
    is_neg = thr == NEG_INF_KEY
    thr_eff = jnp.where(is_neg, NEG_INF_KEY + 1, thr)
    cge = count_ge(thr)
    tied = jnp.logical_and(jnp.logical_not(is_neg), cge > n_sel)
    any_tied = jnp.max(jnp.where(tied, 1.0, 0.0)) > 0.0

    @pl.when(jnp.logical_not(any_tied))
    def _():
        def body(c, carry):
            k0 = pl.multiple_of(c * tk, tk)
            kk = key_ref[:, pl.ds(k0, tk)]
            thr_t = jnp.concatenate([thr_eff] * nlane, axis=1)
            mask_ref[:, pl.ds(k0, tk)] = jnp.where(kk >= thr_t, 1, 0).astype(jnp.int8)
            return carry
        lax.fori_loop(0, nc, body, 0)

    @pl.when(any_tied)
    def _():
        cgt = count_ge(thr + 1)
        need = jnp.where(is_neg[:, :1], 0.0, n_sel - cgt[:, :1])
        upper = (lax.broadcasted_iota(I32, (tk, tk), 0)
                 < lax.broadcasted_iota(I32, (tk, tk), 1)).astype(BF16)

        def body(c, carry):
            k0 = pl.multiple_of(c * tk, tk)
            kk = key_ref[:, pl.ds(k0, tk)]
            thr_t = jnp.concatenate([thr] * nlane, axis=1)
            eq = kk == thr_t
            eqf = jnp.where(eq, 1.0, 0.0)
            before = jnp.dot(eqf.astype(BF16), upper, preferred_element_type=F32) + carry
            keep = jnp.logical_or(kk > thr_t, jnp.logical_and(eq, before < need))
            mask_ref[:, pl.ds(k0, tk)] = jnp.where(keep, 1, 0).astype(jnp.int8)
            return carry + jnp.sum(eqf, axis=1, keepdims=True)
        lax.fori_loop(0, nc, body, jnp.zeros((tq, 1), F32))

    def zero_chunk(c, carry):
        k0 = pl.multiple_of(c * tk, tk)
        mask_ref[:, pl.ds(k0, tk)] = jnp.zeros((tq, tk), jnp.int8)
        return carry
    lax.fori_loop(nc, s_total // tk, zero_chunk, 0)


def _prompt_select(iq, ikt, iw, n_sel):
    s = iq.shape[0]
    tq = _tile(s, 256)
    tk = _tile(tq, 256)
    ck = 2 * tk if s % (2 * tk) == 0 else tk
    assert s // LANES <= 256, "per-lane key counts must stay exact in bf16"
    kern = functools.partial(_idx_kernel, tq=tq, tk=tk, ck=ck, rg=_tile(tq, 64), n_sel=n_sel)
    return pl.pallas_call(
        kern,
        grid=(s // tq,),
        in_specs=[pl.BlockSpec((tq, N_IDX_HEADS * IDX_DIM), lambda i: (i, 0)),
                  pl.BlockSpec((IDX_DIM, s), lambda i: (0, 0)),
                  pl.BlockSpec((tq, N_IDX_HEADS), lambda i: (i, 0))],
        out_specs=pl.BlockSpec((tq, s), lambda i: (i, 0)),
        out_shape=jax.ShapeDtypeStruct((s, s), jnp.int8),
        scratch_shapes=[pltpu.VMEM((tq, s), I32),
                        pltpu.VMEM((N_IDX_HEADS, tq, LANES), F32)],
        compiler_params=_params("arbitrary"),
        name="prompt_select",
    )(iq, ikt, iw)


def _attn_kernel(ii_ref, jj_ref, q_ref, k_ref, v_ref, mask_ref, bt_ref, o_ref,
                 m_ref, l_ref, acc_ref, *, n_heads, blk):
    step = pl.program_id(0)
    i = ii_ref[step]
    j = jj_ref[step]
    nb = blk // LANES
    scale = HEAD_DIM ** -0.5

    @pl.when(j == 0)
    def _():
        m_ref[...] = jnp.full(m_ref.shape, NEG_BIG, F32)
        l_ref[...] = jnp.zeros(l_ref.shape, F32)
        acc_ref[...] = jnp.zeros(acc_ref.shape, F32)

    def bias_block(mode, h):
        zero = jnp.zeros((LANES, LANES), F32)
        rows = []
        for a in range(nb):
            cols = []
            for b in range(nb):
                dd = a - b if mode == "diag" else nb + a - b
                cols.append(bt_ref[0, h] if dd == 0 else bt_ref[1, h] if dd == 1 else zero)
            rows.append(jnp.concatenate(cols, axis=1) if nb > 1 else cols[0])
        return jnp.concatenate(rows, axis=0) if nb > 1 else rows[0]

    def run(mode):
        mbias = jnp.where(mask_ref[...].astype(F32) > 0.0, 0.0, -jnp.inf)
        m_all, l_all, acc_all = m_ref[...], l_ref[...], acc_ref[...]
        m_out, l_out, acc_out = [], [], []
        ones = jnp.ones((blk, HEAD_DIM), BF16)
        for h in range(n_heads):
            sl = slice(h * HEAD_DIM, (h + 1) * HEAD_DIM)
            s = lax.dot_general(q_ref[:, sl], k_ref[:, sl], (((1,), (1,)), ((), ())),
                                preferred_element_type=F32) * scale
            s = s + mbias if mode == "far" else s + (mbias + bias_block(mode, h))
            m_new = jnp.maximum(m_all[h], jnp.max(s, axis=1, keepdims=True))
            alpha = jnp.exp(m_all[h] - m_new)
            p = jnp.exp(s - m_new)
            pv = jnp.dot(p.astype(BF16), jnp.concatenate([v_ref[:, sl], ones], axis=1),
                         preferred_element_type=F32)
            l_out.append(alpha * l_all[h] + pv[:, HEAD_DIM:HEAD_DIM + 1])
            acc_out.append(alpha * acc_all[h] + pv[:, :HEAD_DIM])
            m_out.append(m_new)
        m_ref[...] = jnp.stack(m_out)
        l_ref[...] = jnp.stack(l_out)
        acc_ref[...] = jnp.stack(acc_out)

    @pl.when(i == j)
    def _():
        run("diag")

    @pl.when(i == j + 1)
    def _():
        run("sub")

    @pl.when(i > j + 1)
    def _():
        run("far")

    @pl.when(i == j)
    def _():
        for h in range(n_heads):
            o_ref[:, h * HEAD_DIM:(h + 1) * HEAD_DIM] = (acc_ref[h] / l_ref[h]).astype(o_ref.dtype)


def _prompt_attention(q, k, v, mask, bias_tiles):
    s, width = q.shape
    n_heads = width // HEAD_DIM
    blk = _tile(s, 512)
    nblk = s // blk
    ii = np.concatenate([np.full(i + 1, i, np.int32) for i in range(nblk)])
    jj = np.concatenate([np.arange(i + 1, dtype=np.int32) for i in range(nblk)])
    kern = functools.partial(_attn_kernel, n_heads=n_heads, blk=blk)
    grid_spec = pltpu.PrefetchScalarGridSpec(
        num_scalar_prefetch=2,
        grid=(len(ii),),
        in_specs=[pl.BlockSpec((blk, width), lambda t, ii, jj: (ii[t], 0)),
                  pl.BlockSpec((blk, width), lambda t, ii, jj: (jj[t], 0)),
                  pl.BlockSpec((blk, width), lambda t, ii, jj: (jj[t], 0)),
                  pl.BlockSpec((blk, blk), lambda t, ii, jj: (ii[t], jj[t])),
                  pl.BlockSpec((2, n_heads, LANES, LANES), lambda t, ii, jj: (0, 0, 0, 0))],
        out_specs=pl.BlockSpec((blk, width), lambda t, ii, jj: (ii[t], 0)),
        scratch_shapes=[pltpu.VMEM((n_heads, blk, 1), F32),
                        pltpu.VMEM((n_heads, blk, 1), F32),
                        pltpu.VMEM((n_heads, blk, HEAD_DIM), F32)],
    )
    return pl.pallas_call(
        kern,
        grid_spec=grid_spec,
        out_shape=jax.ShapeDtypeStruct((s, width), BF16),
        compiler_params=_params("arbitrary"),
        name="prompt_attention",
    )(jnp.asarray(ii), jnp.asarray(jj), q, k, v, mask, bias_tiles)


def _sidx_kernel(pt_ref, iq_ref, iw_ref, cache_ref, o_ref, buf_ref, sem_ref, *, chunk_pages):
    b = pl.program_id(0)
    nb = pl.num_programs(0)
    n_pages, page = buf_ref.shape[1], buf_ref.shape[2]

    def page_copy(bb, slot, p):
        return pltpu.make_async_copy(cache_ref.at[0, pt_ref[bb, p]], buf_ref.at[slot, p], sem_ref.at[slot])

    def start_all(bb, slot):
        def body(p, carry):
            page_copy(bb, slot, p).start()
            return carry
        lax.fori_loop(0, n_pages, body, 0)

    @pl.when(b == 0)
    def _():
        start_all(0, 0)

    @pl.when(b + 1 < nb)
    def _():
        start_all(b + 1, (b + 1) % 2)

    slot = b % 2

    def wait_body(p, carry):
        page_copy(b, slot, p).wait()
        return carry
    lax.fori_loop(0, n_pages, wait_body, 0)

    iq = iq_ref[0]
    w = iw_ref[0]
    ck = chunk_pages * page

    def chunk(c, carry):
        p0 = pl.multiple_of(c * chunk_pages, chunk_pages)
        kb = buf_ref[slot, pl.ds(p0, chunk_pages)].reshape(ck, IDX_DIM).astype(BF16)
        d = lax.dot_general(iq, kb, (((1,), (1,)), ((), ())), preferred_element_type=F32)
        sc = jnp.sum(jnp.maximum(d, 0.0) * w, axis=0, keepdims=True) + 0.0
        o_ref[0, :, pl.ds(pl.multiple_of(c * ck, ck), ck)] = sc
        return carry
    lax.fori_loop(0, n_pages // chunk_pages, chunk, 0)


def _sample_scores(iq, iw, cache_idx_k, page_table):
    nb, n_pages = page_table.shape
    page = cache_idx_k.shape[2]
    chunk_pages = _tile(n_pages, 8)
    kern = functools.partial(_sidx_kernel, chunk_pages=chunk_pages)
    grid_spec = pltpu.PrefetchScalarGridSpec(
        num_scalar_prefetch=1,
        grid=(nb,),
        in_specs=[pl.BlockSpec((1, N_IDX_HEADS, IDX_DIM), lambda b, pt: (b, 0, 0)),
                  pl.BlockSpec((1, N_IDX_HEADS, 1), lambda b, pt: (b, 0, 0)),
                  pl.BlockSpec(memory_space=pl.ANY)],
        out_specs=pl.BlockSpec((1, 1, n_pages * page), lambda b, pt: (b, 0, 0)),
        scratch_shapes=[pltpu.VMEM((2, n_pages, page, IDX_DIM), F32),
                        pltpu.SemaphoreType.DMA((2,))],
    )
    return pl.pallas_call(
        kern,
        grid_spec=grid_spec,
        out_shape=jax.ShapeDtypeStruct((nb, 1, n_pages * page), F32),
        compiler_params=_params("arbitrary"),
        name="sample_scores",
    )(page_table, iq, iw, cache_idx_k)


def _sthr_kernel(sc_ref, iq_ref, iw_ref, ikn_ref, idx_ref, nkeep_ref, seln_ref, key_ref, slot_ref,
                 *, n_sel, n_slot, tk):
    nb, past = sc_ref.shape
    nc = past // tk
    nlane = tk // LANES

    prod = iq_ref[...].astype(F32) * ikn_ref[...].astype(F32)
    dn = jnp.sum(prod, axis=2)
    sn = jnp.sum(jnp.maximum(dn, 0.0) * iw_ref[...], axis=1, keepdims=True) + 0.0
    kn = _sort_key(jnp.broadcast_to(sn, (nb, LANES)))

    def to_keys(c, carry):
        k0 = pl.multiple_of(c * tk, tk)
        key_ref[:, pl.ds(k0, tk)] = _sort_key(sc_ref[:, pl.ds(k0, tk)])
        return carry
    lax.fori_loop(0, nc, to_keys, 0)

    def count_ge(cand):
        def body(c, cnt):
            k0 = pl.multiple_of(c * tk, tk)
            kk = key_ref[:, pl.ds(k0, tk)]
            for j in range(nlane):
                cnt = cnt + jnp.where(kk[:, j * LANES:(j + 1) * LANES] >= cand, 1, 0)
            return cnt
        cnt = lax.fori_loop(0, nc, body, jnp.zeros((nb, LANES), I32))
        return (jnp.sum(cnt.astype(F32), axis=1, keepdims=True)
                + jnp.where(kn[:, :1] >= cand[:, :1], 1.0, 0.0))

    thr = _bisect_threshold(count_ge, nb, n_sel)
    cgt = count_ge(thr + 1)
    need = n_sel - cgt
    upper = (lax.broadcasted_iota(I32, (tk, tk), 0)
             < lax.broadcasted_iota(I32, (tk, tk), 1)).astype(BF16)
    slot_iota = lax.broadcasted_iota(I32, (n_slot, tk), 0).astype(F32)
    lane_iota = lax.broadcasted_iota(I32, (n_slot, tk), 1).astype(F32)
    slot_ref[...] = jnp.zeros(slot_ref.shape, F32)

    def body(c, carry):
        n_eq, n_kept = carry
        k0 = pl.multiple_of(c * tk, tk)
        kk = key_ref[:, pl.ds(k0, tk)]
        thr_t = jnp.concatenate([thr] * nlane, axis=1)
        eq = kk == thr_t
        eqf = jnp.where(eq, 1.0, 0.0)
        before = jnp.dot(eqf.astype(BF16), upper, preferred_element_type=F32) + n_eq
        keep = jnp.logical_or(kk > thr_t, jnp.logical_and(eq, before < need))
        keepf = jnp.where(keep, 1.0, 0.0)
        rank = jnp.dot(keepf.astype(BF16), upper, preferred_element_type=F32) + n_kept
        rank = jnp.where(keep, rank, -1.0)
        pos = lane_iota + k0.astype(F32)
        for b in range(nb):
            hit = jnp.broadcast_to(rank[b:b + 1, :], (n_slot, tk)) == slot_iota
            slot_ref[b] += jnp.where(hit, pos, 0.0)
        return (n_eq + jnp.sum(eqf, axis=1, keepdims=True),
                n_kept + jnp.sum(keepf, axis=1, keepdims=True))
    zero = jnp.zeros((nb, 1), F32)
    n_eq, n_kept = lax.fori_loop(0, nc, body, (zero, zero))
    for b in range(nb):
        idx_ref[b] = jnp.sum(slot_ref[b], axis=1, keepdims=True).astype(I32)
    nkeep_ref[...] = jnp.broadcast_to(n_kept, (nb, LANES))
    keep_new = jnp.logical_or(kn > thr, jnp.logical_and(kn == thr, n_eq < need))
    seln_ref[...] = jnp.where(keep_new, 1.0, 0.0)


def _sample_select(scores, iq, iw, ik_new, n_sel, n_slot):
    nb, past = scores.shape
    tk = _tile(past, 256)
    kern = functools.partial(_sthr_kernel, n_sel=n_sel, n_slot=n_slot, tk=tk)
    return pl.pallas_call(
        kern,
        out_shape=[jax.ShapeDtypeStruct((nb, n_slot, 1), I32),
                   jax.ShapeDtypeStruct((nb, LANES), F32),
                   jax.ShapeDtypeStruct((nb, LANES), F32)],
        scratch_shapes=[pltpu.VMEM((nb, past), I32),
                        pltpu.VMEM((nb, n_slot, tk), F32)],
        compiler_params=_params(),
        name="sample_select",
    )(scores, iq, iw, ik_new)


def _sattn_kernel(pt_ref, idx_ref, q_ref, ck_ref, cv_ref, pos_ref, nkeep_ref, kn_ref, vn_ref, seln_ref,
                  rb_ref, o_ref, kbuf_ref, vbuf_ref, sem_ref, *, n_heads, n_slot, page, past, rows):
    b = pl.program_id(0)
    nb = pl.num_programs(0)
    scale = HEAD_DIM ** -0.5

    def row_copies(bb, slot, r):
        s = idx_ref[bb, r]
        phys = pt_ref[bb, s // page]
        off = s % page
        dst = pl.ds(r * n_heads, n_heads)
        return (pltpu.make_async_copy(ck_ref.at[0, phys, off], kbuf_ref.at[slot, dst], sem_ref.at[0, slot]),
                pltpu.make_async_copy(cv_ref.at[0, phys, off], vbuf_ref.at[slot, dst], sem_ref.at[1, slot]))

    def start_all(bb, slot):
        def body(r, carry):
            ck, cv = row_copies(bb, slot, r)
            ck.start()
            cv.start()
            return carry
        lax.fori_loop(0, n_slot, body, 0)

    @pl.when(b == 0)
    def _():
        start_all(0, 0)

    @pl.when(b + 1 < nb)
    def _():
        start_all(b + 1, (b + 1) % 2)

    slot = b % 2

    def wait_body(r, carry):
        ck, cv = row_copies(b, slot, r)
        ck.wait()
        cv.wait()
        return carry
    lax.fori_loop(0, n_slot, wait_body, 0)

    pos = pos_ref[0]
    bucket = _t5_bucket(past - pos)
    valid = lax.broadcasted_iota(I32, (1, n_slot), 1).astype(F32) < nkeep_ref[0, :, 0:1]
    vnew = seln_ref[0, :, 0:1] > 0.0
    for h in range(n_heads):
        sl = slice(h * HEAD_DIM, (h + 1) * HEAD_DIM)
        bias = jnp.zeros((1, n_slot), F32)
        for bk in range(N_BUCKETS):
            bias = jnp.where(bucket == bk, rb_ref[bk, h], bias)
        qh = q_ref[0, :, sl]
        kh = kbuf_ref[slot, pl.ds(h, n_slot, stride=n_heads), :].astype(BF16)
        vh = vbuf_ref[slot, pl.ds(h, n_slot, stride=n_heads), :].astype(BF16)
        s = lax.dot_general(jnp.broadcast_to(qh, (rows, HEAD_DIM)), kh, (((1,), (1,)), ((), ())),
                            preferred_element_type=F32) * scale
        s = jnp.where(valid, s + bias, -jnp.inf)
        knh = kn_ref[0, :, sl].astype(BF16).astype(F32)
        vnh = vn_ref[0, :, sl].astype(BF16).astype(F32)
        sn = jnp.sum(qh.astype(F32) * knh, axis=1, keepdims=True) * scale + rb_ref[0, h]
        sn = jnp.where(vnew, sn, -jnp.inf)
        m = jnp.maximum(jnp.max(s, axis=1, keepdims=True), sn)
        pr = jnp.exp(s - m)
        pn = jnp.exp(sn - m)
        l = jnp.sum(pr, axis=1, keepdims=True) + pn
        acc = (jnp.dot(pr.astype(BF16), vh, preferred_element_type=F32)
               + pn.astype(BF16).astype(F32) * vnh)
        o_ref[0, :, sl] = (acc[0:1] / l[0:1]).astype(o_ref.dtype)


def _sample_attention(q, k_new, v_new, cache_k, cache_v, idx, nkeep, sel_new, page_table, rel_bias):
    nb, n_pages = page_table.shape
    _, n_phys, page, n_heads, _ = cache_k.shape
    width = n_heads * HEAD_DIM
    n_slot = idx.shape[1]
    rows = 16
    kern = functools.partial(_sattn_kernel, n_heads=n_heads, n_slot=n_slot, page=page,
                             past=n_pages * page, rows=rows)
    per_b = lambda b, pt, ix: (b, 0, 0)
    grid_spec = pltpu.PrefetchScalarGridSpec(
        num_scalar_prefetch=2,
        grid=(nb,),
        in_specs=[pl.BlockSpec((1, 1, width), per_b),
                  pl.BlockSpec(memory_space=pl.ANY),
                  pl.BlockSpec(memory_space=pl.ANY),
                  pl.BlockSpec((1, 1, n_slot), per_b),
                  pl.BlockSpec((1, 1, LANES), per_b),
                  pl.BlockSpec((1, 1, width), per_b),
                  pl.BlockSpec((1, 1, width), per_b),
                  pl.BlockSpec((1, 1, LANES), per_b),
                  pl.BlockSpec(memory_space=pltpu.SMEM)],
        out_specs=pl.BlockSpec((1, 1, width), per_b),
        scratch_shapes=[pltpu.VMEM((2, n_slot * n_heads, HEAD_DIM), F32),
                        pltpu.VMEM((2, n_slot * n_heads, HEAD_DIM), F32),
                        pltpu.SemaphoreType.DMA((2, 2))],
    )
    return pl.pallas_call(
        kern,
        grid_spec=grid_spec,
        out_shape=jax.ShapeDtypeStruct((nb, 1, width), BF16),
        compiler_params=_params("arbitrary"),
        name="sample_attention",
    )(page_table, idx, q, cache_k, cache_v, idx.reshape(nb, 1, n_slot), nkeep.reshape(nb, 1, LANES),
      k_new, v_new, sel_new.reshape(nb, 1, LANES), rel_bias)


def _pool_kernel(u_ref, prev_ref, w_ref, ps_ref, o_ref, *, tm, pos0):
    i = pl.program_id(1)
    cur = u_ref[0]
    prev = jnp.where(i == 0, 0.0, prev_ref[0])
    ext = jnp.concatenate([prev, cur], axis=0)
    gdim = cur.shape[1] // len(POOL_WINDOWS)
    t = pos0 + i * tm + lax.broadcasted_iota(I32, (tm, 1), 0)
    for g, win in enumerate(POOL_WINDOWS):
        cs = slice(g * gdim, (g + 1) * gdim)
        s = ext[:, cs]
        span = 1
        while span < win:
            s = s[span:] + s[:-span]
            span *= 2
        wsum = s[s.shape[0] - tm:]
        cnt = jnp.minimum(t + 1, win).astype(F32)
        pooled = (wsum / cnt - cur[:, cs]).astype(BF16)
        mixed = jnp.dot(pooled, w_ref[g], preferred_element_type=F32)
        o_ref[0, :, cs] = (mixed * ps_ref[:, cs]).astype(o_ref.dtype)


def _pool(u, w_pool, pool_scale, pos0):
    nb, n, width = u.shape
    halo = POOL_CTX + 1
    tm = _tile(n, 512)
    ratio = tm // halo
    kern = functools.partial(_pool_kernel, tm=tm, pos0=pos0)
    return pl.pallas_call(
        kern,
        grid=(nb, n // tm),
        in_specs=[pl.BlockSpec((1, tm, width), lambda b, i: (b, i, 0)),
                  pl.BlockSpec((1, halo, width), lambda b, i: (b, jnp.maximum(i * ratio - 1, 0), 0)),
                  pl.BlockSpec(w_pool.shape, lambda b, i: (0, 0, 0)),
                  pl.BlockSpec((1, width), lambda b, i: (0, 0))],
        out_specs=pl.BlockSpec((1, tm, width), lambda b, i: (b, i, 0)),
        out_shape=jax.ShapeDtypeStruct((nb, n, width), BF16),
        compiler_params=_params("arbitrary", "arbitrary"),
        name="pool",
    )(u, u, w_pool, pool_scale)


def _gateproj_kernel(a_ref, p_ref, wa_ref, wp_ref, ga_ref, gb_ref, o_ref):
    a = jnp.dot(a_ref[...], wa_ref[...], preferred_element_type=F32)
    p = jnp.dot(p_ref[...], wp_ref[...], preferred_element_type=F32)
    o_ref[...] = (_sigmoid(ga_ref[...]) * a + _sigmoid(gb_ref[...]) * p).astype(o_ref.dtype)


def _gateproj(attn, pooled, wa, wp, ga, gb):
    m, ka = attn.shape
    kp = pooled.shape[1]
    n = wa.shape[1]
    tm = _tile(m, 1024)
    tn = _tile(n, 512)
    return pl.pallas_call(
        _gateproj_kernel,
        grid=(m // tm, n // tn),
        in_specs=[pl.BlockSpec((tm, ka), lambda i, j: (i, 0)),
                  pl.BlockSpec((tm, kp), lambda i, j: (i, 0)),
                  pl.BlockSpec((ka, tn), lambda i, j: (0, j)),
                  pl.BlockSpec((kp, tn), lambda i, j: (0, j)),
                  pl.BlockSpec((tm, tn), lambda i, j: (i, j)),
                  pl.BlockSpec((tm, tn), lambda i, j: (i, j))],
        out_specs=pl.BlockSpec((tm, tn), lambda i, j: (i, j)),
        out_shape=jax.ShapeDtypeStruct((m, n), BF16),
        compiler_params=_params("arbitrary", "arbitrary"),
        name="gateproj",
    )(attn, pooled, wa, wp, ga, gb)


def _mixout_kernel(m_ref, w_ref, x_ref, gate_ref, gpost_ref, gpre_ref, sc_ref, sh_ref, x1_ref, h2_ref):
    y = jnp.dot(m_ref[...], w_ref[...], preferred_element_type=F32)
    x1 = x_ref[...] + gate_ref[...] * _rms(y, gpost_ref[...])
    x1_ref[...] = x1
    h2_ref[...] = (_rms(x1, gpre_ref[...]) * (1 + sc_ref[...]) + sh_ref[...]).astype(h2_ref.dtype)


def _mixout(m, w_out, x, gate1, g_post, g_pre_ffn, scale2, shift2):
    t, d = x.shape
    tm = _tile(t, 256)
    row = pl.BlockSpec((1, d), lambda i: (0, 0))
    tok = pl.BlockSpec((tm, d), lambda i: (i, 0))
    return pl.pallas_call(
        _mixout_kernel,
        grid=(t // tm,),
        in_specs=[tok, pl.BlockSpec((d, d), lambda i: (0, 0)), tok, _mod_spec(gate1, tm, d),
                  row, row, _mod_spec(scale2, tm, d), _mod_spec(shift2, tm, d)],
        out_specs=[tok, tok],
        out_shape=[jax.ShapeDtypeStruct((t, d), F32), jax.ShapeDtypeStruct((t, d), BF16)],
        compiler_params=_params("arbitrary"),
        name="mixout",
    )(m, w_out, x, gate1, g_post, g_pre_ffn, scale2, shift2)


def _route(h_ref, wt_ref, b_ref, tm):
    n_exp = wt_ref.shape[0]
    per_group = n_exp // N_GROUP
    logits = lax.dot_general(wt_ref[...], h_ref[...], (((1,), (1,)), ((), ())),
                             preferred_element_type=F32)
    s = _sigmoid(logits)
    sb = s + b_ref[...]
    gidx = lax.broadcasted_iota(I32, (per_group, tm), 0)
    gscores = []
    for g in range(N_GROUP):
        x = sb[g * per_group:(g + 1) * per_group]
        m1 = jnp.max(x, axis=0, keepdims=True)
        i1 = jnp.min(jnp.where(x == m1, gidx, per_group), axis=0, keepdims=True)
        m2 = jnp.max(jnp.where(gidx == i1, -jnp.inf, x), axis=0, keepdims=True)
        gscores.append(m1 + m2)
    cur = jnp.concatenate(gscores, axis=0)
    ridx = lax.broadcasted_iota(I32, (N_GROUP, tm), 0)
    gsel = jnp.zeros((N_GROUP, tm), F32)
    for _ in range(TOPK_GROUP):
        m = jnp.max(cur, axis=0, keepdims=True)
        im = jnp.min(jnp.where(cur == m, ridx, N_GROUP), axis=0, keepdims=True)
        pick = ridx == im
        gsel = jnp.where(pick, 1.0, gsel)
        cur = jnp.where(pick, -jnp.inf, cur)
    emask = jnp.concatenate(
        [jnp.broadcast_to(gsel[g:g + 1], (per_group, tm)) for g in range(N_GROUP)], axis=0)
    cur = jnp.where(emask > 0.0, sb, -jnp.inf)
    eidx = lax.broadcasted_iota(I32, (n_exp, tm), 0)
    esel = jnp.zeros((n_exp, tm), F32)
    picks = []
    for _ in range(TOP_K):
        m = jnp.max(cur, axis=0, keepdims=True)
        im = jnp.min(jnp.where(cur == m, eidx, n_exp), axis=0, keepdims=True)
        pick = eidx == im
        esel = jnp.where(pick, 1.0, esel)
        cur = jnp.where(pick, -jnp.inf, cur)
        picks.append(im)
    w = jnp.where(esel > 0.0, s, 0.0)
    gates = w / jnp.sum(w, axis=0, keepdims=True) * ROUTED_SCALE
    return gates, esel, picks


def _router_kernel(h_ref, wt_ref, b_ref, o_ref):
    gates, _, _ = _route(h_ref, wt_ref, b_ref, o_ref.shape[1])
    o_ref[...] = gates


def _router(h, w_router_t, b_router):
    t, d = h.shape
    n_exp = w_router_t.shape[0]
    tm = _tile(t, 512)
    return pl.pallas_call(
        _router_kernel,
        grid=(t // tm,),
        in_specs=[pl.BlockSpec((tm, d), lambda i: (i, 0)),
                  pl.BlockSpec((n_exp, d), lambda i: (0, 0)),
                  pl.BlockSpec((n_exp, 1), lambda i: (0, 0))],
        out_specs=pl.BlockSpec((n_exp, tm), lambda i: (0, i)),
        out_shape=jax.ShapeDtypeStruct((n_exp, t), F32),
        compiler_params=_params("arbitrary"),
        name="router",
    )(h, w_router_t, b_router)


def _router_dispatch_kernel(h_ref, wt_ref, b_ref, eid_ref, pos_ref, wt_out_ref, cnt_ref):
    tm = eid_ref.shape[1]
    n_exp = wt_ref.shape[0]
    gates, esel, picks = _route(h_ref, wt_ref, b_ref, tm)

    @pl.when(pl.program_id(0) == 0)
    def _():
        cnt_ref[...] = jnp.zeros(cnt_ref.shape, F32)

    upper = (lax.broadcasted_iota(I32, (tm, tm), 0)
             < lax.broadcasted_iota(I32, (tm, tm), 1)).astype(BF16)
    before = cnt_ref[:, 0:1] + jnp.dot(esel.astype(BF16), upper, preferred_element_type=F32)
    eidx = lax.broadcasted_iota(I32, (n_exp, tm), 0)
    for k, im in enumerate(picks):
        hit = eidx == im
        eid_ref[k:k + 1, :] = im
        pos_ref[k:k + 1, :] = jnp.sum(jnp.where(hit, before, 0.0), axis=0, keepdims=True).astype(I32)
        wt_out_ref[k:k + 1, :] = jnp.sum(jnp.where(hit, gates, 0.0), axis=0, keepdims=True)
    cnt_ref[...] = cnt_ref[...] + jnp.sum(esel, axis=1, keepdims=True)


def _router_dispatch(h, w_router_t, b_router):
    t, d = h.shape
    n_exp = w_router_t.shape[0]
    tm = _tile(t, 512)
    pick_spec = pl.BlockSpec((TOP_K, tm), lambda i: (0, i))
    return pl.pallas_call(
        _router_dispatch_kernel,
        grid=(t // tm,),
        in_specs=[pl.BlockSpec((tm, d), lambda i: (i, 0)),
                  pl.BlockSpec((n_exp, d), lambda i: (0, 0)),
                  pl.BlockSpec((n_exp, 1), lambda i: (0, 0))],
        out_specs=[pick_spec, pick_spec, pick_spec, pl.BlockSpec((n_exp, LANES), lambda i: (0, 0))],
        out_shape=[jax.ShapeDtypeStruct((TOP_K, t), I32), jax.ShapeDtypeStruct((TOP_K, t), I32),
                   jax.ShapeDtypeStruct((TOP_K, t), F32), jax.ShapeDtypeStruct((n_exp, LANES), F32)],
        compiler_params=_params("arbitrary"),
        name="router_dispatch",
    )(h, w_router_t, b_router)


def _moe_kernel(h_ref, g_ref, wg_ref, wu_ref, wd_ref, init_ref, x_ref, gate_ref, gpost_ref,
                o_ref, acc_ref, *, final):
    e = pl.program_id(1)
    n_exp = pl.num_programs(1)

    @pl.when(e == 0)
    def _():
        acc_ref[...] = init_ref[...]

    gates = g_ref[...]
    lane = lax.broadcasted_iota(I32, gates.shape, 1)
    gcol = jnp.sum(jnp.where(lane == e, gates, 0.0), axis=1, keepdims=True)

    @pl.when(jnp.max(jnp.abs(gcol)) > 0.0)
    def _():
        hh = h_ref[...]
        a = jnp.dot(hh, wg_ref[0], preferred_element_type=F32)
        b = jnp.dot(hh, wu_ref[0], preferred_element_type=F32)
        act = (a * _sigmoid(a) * b).astype(BF16)
        y = jnp.dot(act, wd_ref[0], preferred_element_type=F32)
        acc_ref[...] += y * gcol

    @pl.when(e == n_exp - 1)
    def _():
        if final:
            o_ref[...] = x_ref[...] + gate_ref[...] * _rms(acc_ref[...], gpost_ref[...])
        else:
            o_ref[...] = acc_ref[...]


def _moe(h, gates, wg, wu, wd, init, x, gate2, g_post, final):
    t, d = h.shape
    n_exp, _, f = wg.shape
    tm = _tile(t, 512)
    tok = pl.BlockSpec((tm, d), lambda i, e: (i, 0))
    kern = functools.partial(_moe_kernel, final=final)
    return pl.pallas_call(
        kern,
        grid=(t // tm, n_exp),
        in_specs=[tok,
                  pl.BlockSpec((tm, gates.shape[1]), lambda i, e: (i, 0)),
                  pl.BlockSpec((1, d, f), lambda i, e: (e, 0, 0)),
                  pl.BlockSpec((1, d, f), lambda i, e: (e, 0, 0)),
                  pl.BlockSpec((1, f, d), lambda i, e: (e, 0, 0)),
                  tok, tok, _mod_spec(gate2, tm, d),
                  pl.BlockSpec((1, d), lambda i, e: (0, 0))],
        out_specs=tok,
        out_shape=jax.ShapeDtypeStruct((t, d), F32),
        scratch_shapes=[pltpu.VMEM((tm, d), F32)],
        compiler_params=_params("arbitrary", "arbitrary"),
        name="moe_final" if final else "moe_shared",
    )(h, gates, wg, wu, wd, init, x, gate2, g_post)


def _slots_kernel(pstart_ref, eid_ref, pos_ref, o_ref):
    eid = eid_ref[...]
    base = jnp.zeros(eid.shape, I32)
    for e in range(pstart_ref.shape[0]):
        base = jnp.where(eid == e, pstart_ref[e], base)
    o_ref[...] = base + pos_ref[...]


def _slots(pstart, eid, pos):
    return pl.pallas_call(
        _slots_kernel,
        in_specs=[pl.BlockSpec(memory_space=pltpu.SMEM),
                  pl.BlockSpec(memory_space=pltpu.VMEM), pl.BlockSpec(memory_space=pltpu.VMEM)],
        out_shape=jax.ShapeDtypeStruct(eid.shape, I32),
        name="slots",
    )(pstart, eid, pos)


def _dispatch_kernel(slot_hbm, h_ref, xs_in, xs_out, slot_smem, sem_ref, *, tm, rt):
    del xs_in
    i = pl.program_id(0)
    t0 = pl.multiple_of(i * tm, tm)
    idx_copy = pltpu.make_async_copy(slot_hbm.at[:, pl.ds(t0, tm)], slot_smem, sem_ref.at[0])
    idx_copy.start()
    idx_copy.wait()

    def row_copy(k, t):
        src = h_ref.at[pl.ds(t * rt, rt)]
        dst = xs_out.at[pl.ds(slot_smem[k, t] * rt, rt)]
        return pltpu.make_async_copy(src, dst, sem_ref.at[1])

    def start(t, carry):
        for k in range(TOP_K):
            row_copy(k, t).start()
        return carry
    lax.fori_loop(0, tm, start, 0)

    def wait(t, carry):
        for k in range(TOP_K):
            row_copy(k, t).wait()
        return carry
    lax.fori_loop(0, tm, wait, 0)


def _dispatch(slot, h_rows, n_slots):
    t = slot.shape[1]
    rt = h_rows.shape[0] // t
    tm = _tile(t, 256)
    xs0 = jnp.zeros((n_slots * rt, LANES), F32)
    kern = functools.partial(_dispatch_kernel, tm=tm, rt=rt)
    return pl.pallas_call(
        kern,
        grid=(t // tm,),
        in_specs=[pl.BlockSpec(memory_space=pl.ANY),
                  pl.BlockSpec((tm * rt, LANES), lambda i: (i, 0)),
                  pl.BlockSpec(memory_space=pl.ANY)],
        out_specs=pl.BlockSpec(memory_space=pl.ANY),
        out_shape=jax.ShapeDtypeStruct(xs0.shape, F32),
        scratch_shapes=[pltpu.SMEM((TOP_K, tm), I32), pltpu.SemaphoreType.DMA((2,))],
        input_output_aliases={2: 0},
        compiler_params=_params("arbitrary"),
        name="dispatch",
    )(slot, h_rows, xs0)


def _expert_kernel(te_ref, nu_ref, x_ref, wg_ref, wu_ref, wd_ref, o_ref, *, ts, rt):
    s = pl.program_id(0)

    @pl.when(s < nu_ref[0])
    def _():
        x = jnp.concatenate([x_ref[pl.ds(j, ts, stride=rt), :] for j in range(rt)],
                            axis=1).astype(BF16)
        a = jnp.dot(x, wg_ref[0], preferred_element_type=F32)
        b = jnp.dot(x, wu_ref[0], preferred_element_type=F32)
        act = (a * _sigmoid(a) * b).astype(BF16)
        y = jnp.dot(act, wd_ref[0], preferred_element_type=F32)
        for j in range(rt):
            o_ref[pl.ds(j, ts, stride=rt), :] = y[:, j * LANES:(j + 1) * LANES]

    @pl.when(s >= nu_ref[0])
    def _():
        o_ref[...] = jnp.zeros(o_ref.shape, F32)


def _experts(tile_e, n_used, xs, wg, wu, wd, ts):
    n_exp, d, f = wg.shape
    n_tiles = tile_e.shape[0]
    rt = d // LANES
    kern = functools.partial(_expert_kernel, ts=ts, rt=rt)
    grid_spec = pltpu.PrefetchScalarGridSpec(
        num_scalar_prefetch=2,
        grid=(n_tiles,),
        in_specs=[pl.BlockSpec((ts * rt, LANES), lambda s, te, nu: (s, 0)),
                  pl.BlockSpec((1, d, f), lambda s, te, nu: (te[s], 0, 0)),
                  pl.BlockSpec((1, d, f), lambda s, te, nu: (te[s], 0, 0)),
                  pl.BlockSpec((1, f, d), lambda s, te, nu: (te[s], 0, 0))],
        out_specs=pl.BlockSpec((ts * rt, LANES), lambda s, te, nu: (s, 0)),
    )
    return pl.pallas_call(
        kern,
        grid_spec=grid_spec,
        out_shape=jax.ShapeDtypeStruct(xs.shape, F32),
        compiler_params=_params("arbitrary"),
        name="experts",
    )(tile_e, n_used, xs, wg, wu, wd)


def _combine_kernel(slot_hbm, ys_hbm, wt_ref, init_ref, x_ref, gate_ref, gpost_ref, o_ref,
                    slot_smem, buf_ref, sem_ref, *, tm, rt):
    i = pl.program_id(0)
    t0 = pl.multiple_of(i * tm, tm)
    idx_copy = pltpu.make_async_copy(slot_hbm.at[:, pl.ds(t0, tm)], slot_smem, sem_ref.at[0])
    idx_copy.start()
    idx_copy.wait()

    def row_copy(k, t):
        src = ys_hbm.at[pl.ds(slot_smem[k, t] * rt, rt)]
        dst = buf_ref.at[pl.ds((k * tm + t) * rt, rt)]
        return pltpu.make_async_copy(src, dst, sem_ref.at[1])

    def start(t, carry):
        for k in range(TOP_K):
            row_copy(k, t).start()
        return carry
    lax.fori_loop(0, tm, start, 0)

    def wait(t, carry):
        for k in range(TOP_K):
            row_copy(k, t).wait()
        return carry
    lax.fori_loop(0, tm, wait, 0)

    wts = wt_ref[...]
    cols = []
    for j in range(rt):
        acc = init_ref[:, j * LANES:(j + 1) * LANES]
        for k in range(TOP_K):
            part = buf_ref[pl.ds(k * tm * rt + j, tm, stride=rt), :]
            acc = acc + part * wts[:, k:k + 1]
        cols.append(acc)
    y = jnp.concatenate(cols, axis=1)
    o_ref[...] = x_ref[...] + gate_ref[...] * _rms(y, gpost_ref[...])


def _combine(slot, ys, wts, init, x, gate2, g_post):
    t, d = x.shape
    tm = _tile(t, 128)
    tok = pl.BlockSpec((tm, d), lambda i: (i, 0))
    rt = d // LANES
    kern = functools.partial(_combine_kernel, tm=tm, rt=rt)
    return pl.pallas_call(
        kern,
        grid=(t // tm,),
        in_specs=[pl.BlockSpec(memory_space=pl.ANY), pl.BlockSpec(memory_space=pl.ANY),
                  pl.BlockSpec((tm, TOP_K), lambda i: (i, 0)), tok, tok, _mod_spec(gate2, tm, d),
                  pl.BlockSpec((1, d), lambda i: (0, 0))],
        out_specs=tok,
        out_shape=jax.ShapeDtypeStruct((t, d), F32),
        scratch_shapes=[pltpu.SMEM((TOP_K, tm), I32),
                        pltpu.VMEM((TOP_K * tm * rt, LANES), F32),
                        pltpu.SemaphoreType.DMA((2,))],
        compiler_params=_params("arbitrary"),
        name="combine",
    )(slot, ys, wts, init, x, gate2, g_post)


def _routed_by_dispatch(h2, x1, shared, g2, wts):
    t, d = h2.shape
    rt = d // LANES
    n_exp = wts["w_router_t"].shape[0]
    ts = 256
    eid, pos, pick_w, counts = _router_dispatch(h2, wts["w_router_t"], wts["b_router"])
    cnt = counts[:, 0].astype(I32)
    padded = (cnt + ts - 1) // ts * ts
    pend = jnp.cumsum(padded)
    pstart = pend - padded
    n_tiles = (t * TOP_K) // ts + n_exp
    tile_start = jnp.arange(n_tiles, dtype=I32) * ts
    tile_e = jnp.minimum(jnp.sum((pend[None, :] <= tile_start[:, None]).astype(I32), axis=1), n_exp - 1)
    n_used = (pend[-1:] // ts).astype(I32)
    slot = _slots(pstart, eid, pos)
    h_rows = h2.astype(F32).reshape(t * rt, LANES)
    xs = _dispatch(slot, h_rows, n_tiles * ts)
    ys = _experts(tile_e, n_used, xs, wts["w_gate_e"], wts["w_up_e"], wts["w_down_e"], ts)
    return _combine(slot, ys, pick_w.T, shared, x1, g2, wts["g_post_ffn"])


def _token_path(x, mods, wts):
    sh1, sc1 = mods[0], mods[1]
    h = _normmod(x, wts["g_pre_mix"], sc1, sh1)
    q = _mm(h, wts["w_q"], BF16)
    k = _mm(h, wts["w_k"], F32)
    v = _mm(h, wts["w_v"], F32)
    iq = _mm(h, wts["w_iq"], BF16)
    u = _mm(h, wts["w_u"], F32)
    ga = _mm(h, wts["w_ga"], F32)
    gb = _mm(h, wts["w_gb"], F32)
    ik, iw = _idxproj(h, wts["w_idx"], wts["idx_k_norm"])
    return q, k, v, iq, iw, ik, u, ga, gb


def _back_half(x, attn, pooled, ga, gb, mods, wts, dispatch):
    _, _, g1, sh2, sc2, g2 = mods
    m = _gateproj(attn, pooled, wts["w_proj_attn"], wts["w_proj_pool"], ga, gb)
    x1, h2 = _mixout(m, wts["w_out"], x, g1, wts["g_post_mix"], wts["g_pre_ffn"], sc2, sh2)
    t, d = x.shape
    ones = jnp.ones((t, 1), F32)
    zeros = jnp.zeros((t, d), F32)
    shared = _moe(h2, ones, wts["w_gate_s"], wts["w_up_s"], wts["w_down_s"], zeros, x1, g2,
                  wts["g_post_ffn"], final=False)
    if dispatch:
        return _routed_by_dispatch(h2, x1, shared, g2, wts)
    gates = _router(h2, wts["w_router_t"], wts["b_router"]).T
    return _moe(h2, gates, wts["w_gate_e"], wts["w_up_e"], wts["w_down_e"], shared, x1, g2,
                wts["g_post_ffn"], final=True)


def kernel(x_prompt, x_sample, c_prompt, c_sample, cache_k, cache_v, cache_idx_k, state_pool, page_table, w_ada, b_ada, g_pre_mix, w_in, idx_k_norm, rel_bias, w_pool, pool_scale, w_proj_attn, w_proj_pool, w_out, g_post_mix, g_pre_ffn, w_router, b_router, w_gate_e, w_up_e, w_down_e, w_gate_s, w_up_s, w_down_s, g_post_ffn):
    depth = w_ada.shape[0]
    assert depth == 1, "single-layer trunk"
    nbp, seq, d = x_prompt.shape
    assert nbp == 1, "one prompt sequence"
    nbs, dec_seq, _ = x_sample.shape
    assert dec_seq == 1, "one new token per sample sequence"
    _, n_phys, page, n_heads, head_dim = cache_k.shape
    assert head_dim == HEAD_DIM and page == LANES
    aw = n_heads * HEAD_DIM
    pw = state_pool.shape[-1]
    n_pages = page_table.shape[1]
    past = n_pages * page
    iqw = N_IDX_HEADS * IDX_DIM

    win = w_in[0]
    offs = np.cumsum([0, aw, aw, aw, iqw, IDX_DIM, N_IDX_HEADS, pw, d, d])
    seg = lambda a: win[:, offs[a]:offs[a + 1]].astype(BF16)
    w_idx = jnp.pad(win[:, offs[4]:offs[6]], ((0, 0), (0, LANES - IDX_DIM - N_IDX_HEADS))).astype(BF16)
    wts = dict(
        g_pre_mix=g_pre_mix, idx_k_norm=idx_k_norm,
        w_q=seg(0), w_k=seg(1), w_v=seg(2), w_iq=seg(3), w_idx=w_idx, w_u=seg(6), w_ga=seg(7), w_gb=seg(8),
        w_proj_attn=w_proj_attn[0].astype(BF16), w_proj_pool=w_proj_pool[0].astype(BF16),
        w_out=w_out[0].astype(BF16), g_post_mix=g_post_mix, g_pre_ffn=g_pre_ffn,
        w_router_t=w_router[0].T.astype(BF16), b_router=b_router.reshape(-1, 1),
        w_gate_e=w_gate_e[0].astype(BF16), w_up_e=w_up_e[0].astype(BF16), w_down_e=w_down_e[0].astype(BF16),
        w_gate_s=w_gate_s.astype(BF16), w_up_s=w_up_s.astype(BF16), w_down_s=w_down_s.astype(BF16),
        g_post_ffn=g_post_ffn,
    )
    w_pool_b = w_pool[0].astype(BF16)

    n_c = nbp + nbs
    rows = -(-n_c // SUBLANES) * SUBLANES
    c_all = jnp.pad(jnp.concatenate([c_prompt, c_sample], axis=0), ((0, rows - n_c), (0, 0)))
    mod = _ada(c_all, w_ada[0], b_ada)
    mods_p = [mod[0:1, a * d:(a + 1) * d] for a in range(6)]
    mods_s = [mod[nbp:n_c, a * d:(a + 1) * d] for a in range(6)]

    bias_tiles = _bias_tiles(rel_bias)

    xp = x_prompt.reshape(seq, d)
    q, k, v, iq, iw, ik, u, ga, gb = _token_path(xp, mods_p, wts)
    mask = _prompt_select(iq, ik.astype(BF16).T, iw, min(TOPK_MAX, seq // 4))
    attn = _prompt_attention(q, k.astype(BF16), v.astype(BF16), mask, bias_tiles)
    pooled = _pool(u.reshape(1, seq, pw), w_pool_b, pool_scale, 0).reshape(seq, pw)
    y_prompt = _back_half(xp, attn, pooled, ga, gb, mods_p, wts, True).reshape(nbp, seq, d)
    k_prompt = k.reshape(1, nbp, seq, n_heads, HEAD_DIM)
    v_prompt = v.reshape(1, nbp, seq, n_heads, HEAD_DIM)
    idxk_prompt = ik.reshape(1, nbp, seq, IDX_DIM)
    pool_prompt = u[seq - POOL_CTX:].reshape(1, nbp, POOL_CTX, pw)

    xs = x_sample.reshape(nbs, d)
    qs, ks, vs, iqs, iws, iks, us, gas, gbs = _token_path(xs, mods_s, wts)
    iq3 = iqs.reshape(nbs, N_IDX_HEADS, IDX_DIM)
    iw3 = iws.reshape(nbs, N_IDX_HEADS, 1)
    scores = _sample_scores(iq3, iw3, cache_idx_k, page_table).reshape(nbs, past)
    n_sel_s = min(TOPK_MAX, (past + dec_seq) // 4)
    n_slot = -(-n_sel_s // LANES) * LANES
    idx, nkeep, sel_new = _sample_select(scores, iq3, iws, iks.astype(BF16).reshape(nbs, 1, IDX_DIM),
                                         n_sel_s, n_slot)
    attn_s = _sample_attention(
        qs.reshape(nbs, 1, aw), ks.reshape(nbs, 1, aw), vs.reshape(nbs, 1, aw), cache_k, cache_v,
        idx.reshape(nbs, n_slot), nkeep, sel_new, page_table, rel_bias).reshape(nbs, aw)
    ext = jnp.concatenate([state_pool[0], us.reshape(nbs, 1, pw)], axis=1)
    pooled_s = _pool(ext, w_pool_b, pool_scale, past - POOL_CTX)[:, POOL_CTX]
    y_sample = _back_half(xs, attn_s, pooled_s, gas, gbs, mods_s, wts, False).reshape(nbs, 1, d)
    k_sample = ks.reshape(1, nbs, 1, n_heads, HEAD_DIM)
    v_sample = vs.reshape(1, nbs, 1, n_heads, HEAD_DIM)
    idxk_sample = iks.reshape(1, nbs, 1, IDX_DIM)
    pool_sample = ext[:, 1:].reshape(1, nbs, POOL_CTX, pw)

    return (y_prompt, y_sample, k_prompt, v_prompt, idxk_prompt, pool_prompt,
            k_sample, v_sample, idxk_sample, pool_sample)
```

```python
import functools
import math

import jax
import jax.numpy as jnp
import numpy as np
from jax import lax
from jax.experimental import pallas as pl
from jax.experimental.pallas import tpu as pltpu

F32 = jnp.float32
BF16 = jnp.bfloat16
I32 = jnp.int32

HEAD_DIM = 128
N_IDX_HEADS = 16
IDX_DIM = 64
TOPK_MAX = 256
N_BUCKETS = 32
MAX_DISTANCE = 128
POOL_WINDOWS = (2, 4, 8, 16)
POOL_CTX = max(POOL_WINDOWS) - 1
N_GROUP = 8
TOPK_GROUP = 4
TOP_K = 8
ROUTED_SCALE = 2.5
EPS = 1e-6

LANES = 128
SUBLANES = 8
VMEM_LIMIT_BYTES = 56 * 1024 * 1024

NEG_BIG = -1e30
NEG_INF_KEY = int(np.int32(np.uint32(0xFF800000)) ^ np.int32(0x7FFFFFFF))
INT_MIN = -(2 ** 31)


def _params(*sem):
    return pltpu.CompilerParams(dimension_semantics=sem, vmem_limit_bytes=VMEM_LIMIT_BYTES)


def _tile(n, pref):
    if n <= pref:
        return n
    t = pref
    while n % t:
        t //= 2
    return t


def _rms(x, g):
    return x * lax.rsqrt(jnp.mean(x * x, axis=-1, keepdims=True) + EPS) * g


def _sigmoid(x):
    return jax.nn.sigmoid(x)


def _ada_kernel(c_ref, w_ref, b_ref, o_ref):
    c = c_ref[...]
    a = (c * _sigmoid(c)).astype(BF16)
    o_ref[...] = jnp.dot(a, w_ref[...].astype(BF16), preferred_element_type=F32) + b_ref[...]


def _ada(c, w, b):
    r, d = c.shape
    n = w.shape[1]
    tn = _tile(n, 1024)
    return pl.pallas_call(
        _ada_kernel,
        grid=(n // tn,),
        in_specs=[pl.BlockSpec((r, d), lambda j: (0, 0)),
                  pl.BlockSpec((d, tn), lambda j: (0, j)),
                  pl.BlockSpec((1, tn), lambda j: (0, j))],
        out_specs=pl.BlockSpec((r, tn), lambda j: (0, j)),
        out_shape=jax.ShapeDtypeStruct((r, n), F32),
        compiler_params=_params("arbitrary"),
        name="ada",
    )(c, w, b)


def _normmod_kernel(x_ref, g_ref, sc_ref, sh_ref, o_ref):
    y = _rms(x_ref[...], g_ref[...])
    o_ref[...] = (y * (1 + sc_ref[...]) + sh_ref[...]).astype(o_ref.dtype)


def _mod_spec(mod, tm, d):
    if mod.shape[0] == 1:
        return pl.BlockSpec((1, d), lambda i, *_: (0, 0))
    return pl.BlockSpec((tm, d), lambda i, *_: (i, 0))


def _normmod(x, g, scale, shift):
    t, d = x.shape
    tm = _tile(t, 512)
    return pl.pallas_call(
        _normmod_kernel,
        grid=(t // tm,),
        in_specs=[pl.BlockSpec((tm, d), lambda i: (i, 0)),
                  pl.BlockSpec((1, d), lambda i: (0, 0)),
                  _mod_spec(scale, tm, d), _mod_spec(shift, tm, d)],
        out_specs=pl.BlockSpec((tm, d), lambda i: (i, 0)),
        out_shape=jax.ShapeDtypeStruct((t, d), BF16),
        compiler_params=_params("arbitrary"),
        name="normmod",
    )(x, g, scale, shift)


def _mm_kernel(a_ref, w_ref, o_ref):
    o_ref[...] = jnp.dot(a_ref[...], w_ref[...], preferred_element_type=F32).astype(o_ref.dtype)


def _mm(a, w, out_dtype):
    m, k = a.shape
    n = w.shape[1]
    tm = _tile(m, 1024)
    tn = _tile(n, 1024)
    return pl.pallas_call(
        _mm_kernel,
        grid=(m // tm, n // tn),
        in_specs=[pl.BlockSpec((tm, k), lambda i, j: (i, 0)),
                  pl.BlockSpec((k, tn), lambda i, j: (0, j))],
        out_specs=pl.BlockSpec((tm, tn), lambda i, j: (i, j)),
        out_shape=jax.ShapeDtypeStruct((m, n), out_dtype),
        compiler_params=_params("arbitrary", "arbitrary"),
        name="mm",
    )(a, w)


def _idxproj_kernel(a_ref, w_ref, g_ref, ik_ref, iw_ref):
    z = jnp.dot(a_ref[...], w_ref[...], preferred_element_type=F32)
    ik = z[:, :IDX_DIM]
    xc = ik - jnp.mean(ik, axis=-1, keepdims=True)
    y = xc * lax.rsqrt(jnp.mean(xc * xc, axis=-1, keepdims=True) + EPS)
    ik_ref[...] = y * g_ref[...]
    iw_ref[...] = z[:, IDX_DIM:IDX_DIM + N_IDX_HEADS] * (IDX_DIM ** -0.5 * N_IDX_HEADS ** -0.5)


def _idxproj(a, w, g):
    m, k = a.shape
    tm = _tile(m, 512)
    return pl.pallas_call(
        _idxproj_kernel,
        grid=(m // tm,),
        in_specs=[pl.BlockSpec((tm, k), lambda i: (i, 0)),
                  pl.BlockSpec((k, LANES), lambda i: (0, 0)),
                  pl.BlockSpec((1, IDX_DIM), lambda i: (0, 0))],
        out_specs=[pl.BlockSpec((tm, IDX_DIM), lambda i: (i, 0)),
                   pl.BlockSpec((tm, N_IDX_HEADS), lambda i: (i, 0))],
        out_shape=[jax.ShapeDtypeStruct((m, IDX_DIM), F32),
                   jax.ShapeDtypeStruct((m, N_IDX_HEADS), F32)],
        compiler_params=_params("arbitrary"),
        name="idxproj",
    )(a, w, g)


def _t5_bucket(rel):
    n = jnp.maximum(rel, 0)
    max_exact = N_BUCKETS // 2
    nf = jnp.maximum(n, 1).astype(F32)
    large = max_exact + (jnp.log(nf / max_exact) / math.log(MAX_DISTANCE / max_exact)
                         * (N_BUCKETS - max_exact)).astype(I32)
    large = jnp.minimum(large, N_BUCKETS - 1)
    return jnp.where(n < max_exact, n, large)


def _bias_kernel(rb_ref, o_ref):
    n_heads = o_ref.shape[1]
    r = lax.broadcasted_iota(I32, (LANES, LANES), 0)
    c = lax.broadcasted_iota(I32, (LANES, LANES), 1)
    for d in range(2):
        bucket = _t5_bucket(d * LANES + r - c)
        for h in range(n_heads):
            val = jnp.zeros((LANES, LANES), F32)
            for b in range(N_BUCKETS):
                val = jnp.where(bucket == b, rb_ref[b, h], val)
            o_ref[d, h] = val - rb_ref[N_BUCKETS - 1, h]


def _bias_tiles(rel_bias):
    n_heads = rel_bias.shape[1]
    return pl.pallas_call(
        _bias_kernel,
        in_specs=[pl.BlockSpec(memory_space=pltpu.SMEM)],
        out_shape=jax.ShapeDtypeStruct((2, n_heads, LANES, LANES), F32),
        name="bias_tiles",
    )(rel_bias)


def _sort_key(x):
    bits = pltpu.bitcast(x, I32)
    return jnp.where(bits < 0, bits ^ 0x7FFFFFFF, bits)


def _bisect_threshold(count_ge, rows, n_sel):
    def body(it, prefix):
        bit = 31 - it
        cand = prefix + jnp.left_shift(jnp.int32(1), bit)
        cnt = count_ge(cand)
        return jnp.where(cnt >= n_sel, cand, prefix)

    return lax.fori_loop(0, 32, body, jnp.full((rows, LANES), INT_MIN, I32))


def _idx_kernel(iq_ref, ikt_ref, iw_ref, mask_ref, key_ref, wb_ref, *, tq, tk, ck, rg, n_sel):
    i = pl.program_id(0)
    s_total = ikt_ref.shape[1]
    q0 = i * tq
    nc = (q0 + tq) // tk
    nlane = tk // LANES

    for h in range(N_IDX_HEADS):
        wb_ref[h] = jnp.broadcast_to(iw_ref[:, h:h + 1], (tq, LANES))

    def score_chunk(c, carry):
        k0 = pl.multiple_of(c * tk, tk)
        kt = ikt_ref[:, pl.ds(k0, tk)]
        acc = jnp.zeros((tq, tk), F32)
        for h in range(N_IDX_HEADS):
            d = jnp.dot(iq_ref[:, h * IDX_DIM:(h + 1) * IDX_DIM], kt, preferred_element_type=F32)
            w = jnp.concatenate([wb_ref[h]] * nlane, axis=1)
            acc = acc + jnp.maximum(d, 0.0) * w
        acc = acc + 0.0
        t = q0 + lax.broadcasted_iota(I32, (tq, tk), 0)
        s = k0 + lax.broadcasted_iota(I32, (tq, tk), 1)
        acc = jnp.where(s <= t, acc, -jnp.inf)
        key_ref[:, pl.ds(k0, tk)] = _sort_key(acc)
        return carry

    lax.fori_loop(0, nc, score_chunk, 0)

    ncc = (nc * tk + ck - 1) // ck

    def fill_chunk(c, carry):
        key_ref[:, pl.ds(pl.multiple_of(c * tk, tk), tk)] = jnp.full((tq, tk), NEG_INF_KEY, I32)
        return carry
    lax.fori_loop(nc, ncc * (ck // tk), fill_chunk, 0)

    ones = jnp.ones((LANES, LANES), BF16)

    def count_ge(cand):
        parts = []
        for g in range(tq // rg):
            rows = pl.ds(g * rg, rg)
            cand_g = cand[g * rg:(g + 1) * rg]

            def body(c, cnt, rows=rows, cand_g=cand_g):
                kk = key_ref[rows, pl.ds(pl.multiple_of(c * ck, ck), ck)]
                for j in range(ck // LANES):
                    cnt = cnt + jnp.where(kk[:, j * LANES:(j + 1) * LANES] >= cand_g, 1, 0)
                return cnt
            parts.append(lax.fori_loop(0, ncc, body, jnp.zeros((rg, LANES), I32)))
        cnt = jnp.concatenate(parts, axis=0) if len(parts) > 1 else parts[0]
        return jnp.dot(cnt.astype(F32).astype(BF16), ones, preferred_element_type=F32)

    thr = _bisect_threshold(count_ge, tq, n_sel)
    is_neg = thr == NEG_INF_KEY
    thr_eff = jnp.where(is_neg, NEG_INF_KEY + 1, thr)
    cge = count_ge(thr)
    tied = jnp.logical_and(jnp.logical_not(is_neg), cge > n_sel)
    any_tied = jnp.max(jnp.where(tied, 1.0, 0.0)) > 0.0

    @pl.when(jnp.logical_not(any_tied))
    def _():
        def body(c, carry):
            k0 = pl.multiple_of(c * tk, tk)
            kk = key_ref[:, pl.ds(k0, tk)]
            thr_t = jnp.concatenate([thr_eff] * nlane, axis=1)
            mask_ref[:, pl.ds(k0, tk)] = jnp.where(kk >= thr_t, 1, 0).astype(jnp.int8)
            return carry
        lax.fori_loop(0, nc, body, 0)

    @pl.when(any_tied)
    def _():
        cgt = count_ge(thr + 1)
        need = jnp.where(is_neg[:, :1], 0.0, n_sel - cgt[:, :1])
        upper = (lax.broadcasted_iota(I32, (tk, tk), 0)
                 < lax.broadcasted_iota(I32, (tk, tk), 1)).astype(BF16)

        def body(c, carry):
            k0 = pl.multiple_of(c * tk, tk)
            kk = key_ref[:, pl.ds(k0, tk)]
            thr_t = jnp.concatenate([thr] * nlane, axis=1)
            eq = kk == thr_t
            eqf = jnp.where(eq, 1.0, 0.0)
            before = jnp.dot(eqf.astype(BF16), upper, preferred_element_type=F32) + carry
            keep = jnp.logical_or(kk > thr_t, jnp.logical_and(eq, before < need))
            mask_ref[:, pl.ds(k0, tk)] = jnp.where(keep, 1, 0).astype(jnp.int8)
            return carry + jnp.sum(eqf, axis=1, keepdims=True)
        lax.fori_loop(0, nc, body, jnp.zeros((tq, 1), F32))

    def zero_chunk(c, carry):
        k0 = pl.multiple_of(c * tk, tk)
        mask_ref[:, pl.ds(k0, tk)] = jnp.zeros((tq, tk), jnp.int8)
        return carry
    lax.fori_loop(nc, s_total // tk, zero_chunk, 0)


def _prompt_select(iq, ikt, iw, n_sel):
    s = iq.shape[0]
    tq = _tile(s, 256)
    tk = _tile(tq, 256)
    ck = 2 * tk if s % (2 * tk) == 0 else tk
    assert s // LANES <= 256, "per-lane key counts must stay exact in bf16"
    kern = functools.partial(_idx_kernel, tq=tq, tk=tk, ck=ck, rg=_tile(tq, 64), n_sel=n_sel)
    return pl.pallas_call(
        kern,
        grid=(s // tq,),
        in_specs=[pl.BlockSpec((tq, N_IDX_HEADS * IDX_DIM), lambda i: (i, 0)),
                  pl.BlockSpec((IDX_DIM, s), lambda i: (0, 0)),
                  pl.BlockSpec((tq, N_IDX_HEADS), lambda i: (i, 0))],
        out_specs=pl.BlockSpec((tq, s), lambda i: (i, 0)),
        out_shape=jax.ShapeDtypeStruct((s, s), jnp.int8),
        scratch_shapes=[pltpu.VMEM((tq, s), I32),
                        pltpu.VMEM((N_IDX_HEADS, tq, LANES), F32)],
        compiler_params=_params("arbitrary"),
        name="prompt_select",
    )(iq, ikt, iw)


def _attn_kernel(ii_ref, jj_ref, q_ref, k_ref, v_ref, mask_ref, bt_ref, o_ref,
                 m_ref, l_ref, acc_ref, *, n_heads, blk):
    step = pl.program_id(0)
    i = ii_ref[step]
    j = jj_ref[step]
    nb = blk // LANES
    scale = HEAD_DIM ** -0.5

    @pl.when(j == 0)
    def _():
        m_ref[...] = jnp.full(m_ref.shape, NEG_BIG, F32)
        l_ref[...] = jnp.zeros(l_ref.shape, F32)
        acc_ref[...] = jnp.zeros(acc_ref.shape, F32)

    def bias_block(mode, h):
        zero = jnp.zeros((LANES, LANES), F32)
        rows = []
        for a in range(nb):
            cols = []
            for b in range(nb):
                dd = a - b if mode == "diag" else nb + a - b
                cols.append(bt_ref[0, h] if dd == 0 else bt_ref[1, h] if dd == 1 else zero)
            rows.append(jnp.concatenate(cols, axis=1) if nb > 1 else cols[0])
        return jnp.concatenate(rows, axis=0) if nb > 1 else rows[0]

    def run(mode):
        mbias = jnp.where(mask_ref[...].astype(F32) > 0.0, 0.0, -jnp.inf)
        m_all, l_all, acc_all = m_ref[...], l_ref[...], acc_ref[...]
        m_out, l_out, acc_out = [], [], []
        ones = jnp.ones((blk, HEAD_DIM), BF16)
        for h in range(n_heads):
            sl = slice(h * HEAD_DIM, (h + 1) * HEAD_DIM)
            s = lax.dot_general(q_ref[:, sl], k_ref[:, sl], (((1,), (1,)), ((), ())),
                                preferred_element_type=F32) * scale
            s = s + mbias if mode == "far" else s + (mbias + bias_block(mode, h))
            m_new = jnp.maximum(m_all[h], jnp.max(s, axis=1, keepdims=True))
            alpha = jnp.exp(m_all[h] - m_new)
            p = jnp.exp(s - m_new)
            pv = jnp.dot(p.astype(BF16), jnp.concatenate([v_ref[:, sl], ones], axis=1),
                         preferred_element_type=F32)
            l_out.append(alpha * l_all[h] + pv[:, HEAD_DIM:HEAD_DIM + 1])
            acc_out.append(alpha * acc_all[h] + pv[:, :HEAD_DIM])
            m_out.append(m_new)
        m_ref[...] = jnp.stack(m_out)
        l_ref[...] = jnp.stack(l_out)
        acc_ref[...] = jnp.stack(acc_out)

    @pl.when(i == j)
    def _():
        run("diag")

    @pl.when(i == j + 1)
    def _():
        run("sub")

    @pl.when(i > j + 1)
    def _():
        run("far")

    @pl.when(i == j)
    def _():
        for h in range(n_heads):
            o_ref[:, h * HEAD_DIM:(h + 1) * HEAD_DIM] = (acc_ref[h] / l_ref[h]).astype(o_ref.dtype)


def _prompt_attention(q, k, v, mask, bias_tiles):
    s, width = q.shape
    n_heads = width // HEAD_DIM
    blk = _tile(s, 512)
    nblk = s // blk
    ii = np.concatenate([np.full(i + 1, i, np.int32) for i in range(nblk)])
    jj = np.concatenate([np.arange(i + 1, dtype=np.int32) for i in range(nblk)])
    kern = functools.partial(_attn_kernel, n_heads=n_heads, blk=blk)
    grid_spec = pltpu.PrefetchScalarGridSpec(
        num_scalar_prefetch=2,
        grid=(len(ii),),
        in_specs=[pl.BlockSpec((blk, width), lambda t, ii, jj: (ii[t], 0)),
                  pl.BlockSpec((blk, width), lambda t, ii, jj: (jj[t], 0)),
                  pl.BlockSpec((blk, width), lambda t, ii, jj: (jj[t], 0)),
                  pl.BlockSpec((blk, blk), lambda t, ii, jj: (ii[t], jj[t])),
                  pl.BlockSpec((2, n_heads, LANES, LANES), lambda t, ii, jj: (0, 0, 0, 0))],
        out_specs=pl.BlockSpec((blk, width), lambda t, ii, jj: (ii[t], 0)),
        scratch_shapes=[pltpu.VMEM((n_heads, blk, 1), F32),
                        pltpu.VMEM((n_heads, blk, 1), F32),
                        pltpu.VMEM((n_heads, blk, HEAD_DIM), F32)],
    )
    return pl.pallas_call(
        kern,
        grid_spec=grid_spec,
        out_shape=jax.ShapeDtypeStruct((s, width), BF16),
        compiler_params=_params("arbitrary"),
        name="prompt_attention",
    )(jnp.asarray(ii), jnp.asarray(jj), q, k, v, mask, bias_tiles)


def _sidx_kernel(pt_ref, iq_ref, iw_ref, cache_ref, o_ref, buf_ref, sem_ref, *, chunk_pages):
    b = pl.program_id(0)
    nb = pl.num_programs(0)
    n_pages, page = buf_ref.shape[1], buf_ref.shape[2]

    def page_copy(bb, slot, p):
        return pltpu.make_async_copy(cache_ref.at[0, pt_ref[bb, p]], buf_ref.at[slot, p], sem_ref.at[slot])

    def start_all(bb, slot):
        def body(p, carry):
            page_copy(bb, slot, p).start()
            return carry
        lax.fori_loop(0, n_pages, body, 0)

    @pl.when(b == 0)
    def _():
        start_all(0, 0)

    @pl.when(b + 1 < nb)
    def _():
        start_all(b + 1, (b + 1) % 2)

    slot = b % 2

    def wait_body(p, carry):
        page_copy(b, slot, p).wait()
        return carry
    lax.fori_loop(0, n_pages, wait_body, 0)

    iq = iq_ref[0]
    w = iw_ref[0]
    ck = chunk_pages * page

    def chunk(c, carry):
        p0 = pl.multiple_of(c * chunk_pages, chunk_pages)
        kb = buf_ref[slot, pl.ds(p0, chunk_pages)].reshape(ck, IDX_DIM).astype(BF16)
        d = lax.dot_general(iq, kb, (((1,), (1,)), ((), ())), preferred_element_type=F32)
        sc = jnp.sum(jnp.maximum(d, 0.0) * w, axis=0, keepdims=True) + 0.0
        o_ref[0, :, pl.ds(pl.multiple_of(c * ck, ck), ck)] = sc
        return carry
    lax.fori_loop(0, n_pages // chunk_pages, chunk, 0)


def _sample_scores(iq, iw, cache_idx_k, page_table):
    nb, n_pages = page_table.shape
    page = cache_idx_k.shape[2]
    chunk_pages = _tile(n_pages, 8)
    kern = functools.partial(_sidx_kernel, chunk_pages=chunk_pages)
    grid_spec = pltpu.PrefetchScalarGridSpec(
        num_scalar_prefetch=1,
        grid=(nb,),
        in_specs=[pl.BlockSpec((1, N_IDX_HEADS, IDX_DIM), lambda b, pt: (b, 0, 0)),
                  pl.BlockSpec((1, N_IDX_HEADS, 1), lambda b, pt: (b, 0, 0)),
                  pl.BlockSpec(memory_space=pl.ANY)],
        out_specs=pl.BlockSpec((1, 1, n_pages * page), lambda b, pt: (b, 0, 0)),
        scratch_shapes=[pltpu.VMEM((2, n_pages, page, IDX_DIM), F32),
                        pltpu.SemaphoreType.DMA((2,))],
    )
    return pl.pallas_call(
        kern,
        grid_spec=grid_spec,
        out_shape=jax.ShapeDtypeStruct((nb, 1, n_pages * page), F32),
        compiler_params=_params("arbitrary"),
        name="sample_scores",
    )(page_table, iq, iw, cache_idx_k)


def _sthr_kernel(sc_ref, iq_ref, iw_ref, ikn_ref, idx_ref, nkeep_ref, seln_ref, key_ref, slot_ref,
                 *, n_sel, n_slot, tk):
    nb, past = sc_ref.shape
    nc = past // tk
    nlane = tk // LANES

    prod = iq_ref[...].astype(F32) * ikn_ref[...].astype(F32)
    dn = jnp.sum(prod, axis=2)
    sn = jnp.sum(jnp.maximum(dn, 0.0) * iw_ref[...], axis=1, keepdims=True) + 0.0
    kn = _sort_key(jnp.broadcast_to(sn, (nb, LANES)))

    def to_keys(c, carry):
        k0 = pl.multiple_of(c * tk, tk)
        key_ref[:, pl.ds(k0, tk)] = _sort_key(sc_ref[:, pl.ds(k0, tk)])
        return carry
    lax.fori_loop(0, nc, to_keys, 0)

    def count_ge(cand):
        def body(c, cnt):
            k0 = pl.multiple_of(c * tk, tk)
            kk = key_ref[:, pl.ds(k0, tk)]
            for j in range(nlane):
                cnt = cnt + jnp.where(kk[:, j * LANES:(j + 1) * LANES] >= cand, 1, 0)
            return cnt
        cnt = lax.fori_loop(0, nc, body, jnp.zeros((nb, LANES), I32))
        return (jnp.sum(cnt.astype(F32), axis=1, keepdims=True)
                + jnp.where(kn[:, :1] >= cand[:, :1], 1.0, 0.0))

    thr = _bisect_threshold(count_ge, nb, n_sel)
    cgt = count_ge(thr + 1)
    need = n_sel - cgt
    upper = (lax.broadcasted_iota(I32, (tk, tk), 0)
             < lax.broadcasted_iota(I32, (tk, tk), 1)).astype(BF16)
    slot_iota = lax.broadcasted_iota(I32, (n_slot, tk), 0).astype(F32)
    lane_iota = lax.broadcasted_iota(I32, (n_slot, tk), 1).astype(F32)
    slot_ref[...] = jnp.zeros(slot_ref.shape, F32)

    def body(c, carry):
        n_eq, n_kept = carry
        k0 = pl.multiple_of(c * tk, tk)
        kk = key_ref[:, pl.ds(k0, tk)]
        thr_t = jnp.concatenate([thr] * nlane, axis=1)
        eq = kk == thr_t
        eqf = jnp.where(eq, 1.0, 0.0)
        before = jnp.dot(eqf.astype(BF16), upper, preferred_element_type=F32) + n_eq
        keep = jnp.logical_or(kk > thr_t, jnp.logical_and(eq, before < need))
        keepf = jnp.where(keep, 1.0, 0.0)
        rank = jnp.dot(keepf.astype(BF16), upper, preferred_element_type=F32) + n_kept
        rank = jnp.where(keep, rank, -1.0)
        pos = lane_iota + k0.astype(F32)
        for b in range(nb):
            hit = jnp.broadcast_to(rank[b:b + 1, :], (n_slot, tk)) == slot_iota
            slot_ref[b] += jnp.where(hit, pos, 0.0)
        return (n_eq + jnp.sum(eqf, axis=1, keepdims=True),
                n_kept + jnp.sum(keepf, axis=1, keepdims=True))
    zero = jnp.zeros((nb, 1), F32)
    n_eq, n_kept = lax.fori_loop(0, nc, body, (zero, zero))
    for b in range(nb):
        idx_ref[b] = jnp.sum(slot_ref[b], axis=1, keepdims=True).astype(I32)
    nkeep_ref[...] = jnp.broadcast_to(n_kept, (nb, LANES))
    keep_new = jnp.logical_or(kn > thr, jnp.logical_and(kn == thr, n_eq < need))
    seln_ref[...] = jnp.where(keep_new, 1.0, 0.0)


def _sample_select(scores, iq, iw, ik_new, n_sel, n_slot):
    nb, past = scores.shape
    tk = _tile(past, 256)
    kern = functools.partial(_sthr_kernel, n_sel=n_sel, n_slot=n_slot, tk=tk)
    return pl.pallas_call(
        kern,
        out_shape=[jax.ShapeDtypeStruct((nb, n_slot, 1), I32),
                   jax.ShapeDtypeStruct((nb, LANES), F32),
                   jax.ShapeDtypeStruct((nb, LANES), F32)],
        scratch_shapes=[pltpu.VMEM((nb, past), I32),
                        pltpu.VMEM((nb, n_slot, tk), F32)],
        compiler_params=_params(),
        name="sample_select",
    )(scores, iq, iw, ik_new)


def _sattn_kernel(pt_ref, idx_ref, q_ref, ck_ref, cv_ref, pos_ref, nkeep_ref, kn_ref, vn_ref, seln_ref,
                  rb_ref, o_ref, kbuf_ref, vbuf_ref, sem_ref, *, n_heads, n_slot, page, past, rows):
    b = pl.program_id(0)
    nb = pl.num_programs(0)
    scale = HEAD_DIM ** -0.5

    def row_copies(bb, slot, r):
        s = idx_ref[bb, r]
        phys = pt_ref[bb, s // page]
        off = s % page
        dst = pl.ds(r * n_heads, n_heads)
        return (pltpu.make_async_copy(ck_ref.at[0, phys, off], kbuf_ref.at[slot, dst], sem_ref.at[0, slot]),
                pltpu.make_async_copy(cv_ref.at[0, phys, off], vbuf_ref.at[slot, dst], sem_ref.at[1, slot]))

    def start_all(bb, slot):
        def body(r, carry):
            ck, cv = row_copies(bb, slot, r)
            ck.start()
            cv.start()
            return carry
        lax.fori_loop(0, n_slot, body, 0)

    @pl.when(b == 0)
    def _():
        start_all(0, 0)

    @pl.when(b + 1 < nb)
    def _():
        start_all(b + 1, (b + 1) % 2)

    slot = b % 2

    def wait_body(r, carry):
        ck, cv = row_copies(b, slot, r)
        ck.wait()
        cv.wait()
        return carry
    lax.fori_loop(0, n_slot, wait_body, 0)

    pos = pos_ref[0]
    bucket = _t5_bucket(past - pos)
    valid = lax.broadcasted_iota(I32, (1, n_slot), 1).astype(F32) < nkeep_ref[0, :, 0:1]
    vnew = seln_ref[0, :, 0:1] > 0.0
    for h in range(n_heads):
        sl = slice(h * HEAD_DIM, (h + 1) * HEAD_DIM)
        bias = jnp.zeros((1, n_slot), F32)
        for bk in range(N_BUCKETS):
            bias = jnp.where(bucket == bk, rb_ref[bk, h], bias)
        qh = q_ref[0, :, sl]
        kh = kbuf_ref[slot, pl.ds(h, n_slot, stride=n_heads), :].astype(BF16)
        vh = vbuf_ref[slot, pl.ds(h, n_slot, stride=n_heads), :].astype(BF16)
        s = lax.dot_general(jnp.broadcast_to(qh, (rows, HEAD_DIM)), kh, (((1,), (1,)), ((), ())),
                            preferred_element_type=F32) * scale
        s = jnp.where(valid, s + bias, -jnp.inf)
        knh = kn_ref[0, :, sl].astype(BF16).astype(F32)
        vnh = vn_ref[0, :, sl].astype(BF16).astype(F32)
        sn = jnp.sum(qh.astype(F32) * knh, axis=1, keepdims=True) * scale + rb_ref[0, h]
        sn = jnp.where(vnew, sn, -jnp.inf)
        m = jnp.maximum(jnp.max(s, axis=1, keepdims=True), sn)
        pr = jnp.exp(s - m)
        pn = jnp.exp(sn - m)
        l = jnp.sum(pr, axis=1, keepdims=True) + pn
        acc = (jnp.dot(pr.astype(BF16), vh, preferred_element_type=F32)
               + pn.astype(BF16).astype(F32) * vnh)
        o_ref[0, :, sl] = (acc[0:1] / l[0:1]).astype(o_ref.dtype)


def _sample_attention(q, k_new, v_new, cache_k, cache_v, idx, nkeep, sel_new, page_table, rel_bias):
    nb, n_pages = page_table.shape
    _, n_phys, page, n_heads, _ = cache_k.shape
    width = n_heads * HEAD_DIM
    n_slot = idx.shape[1]
    rows = 16
    kern = functools.partial(_sattn_kernel, n_heads=n_heads, n_slot=n_slot, page=page,
                             past=n_pages * page, rows=rows)
    per_b = lambda b, pt, ix: (b, 0, 0)
    grid_spec = pltpu.PrefetchScalarGridSpec(
        num_scalar_prefetch=2,
        grid=(nb,),
        in_specs=[pl.BlockSpec((1, 1, width), per_b),
                  pl.BlockSpec(memory_space=pl.ANY),
                  pl.BlockSpec(memory_space=pl.ANY),
                  pl.BlockSpec((1, 1, n_slot), per_b),
                  pl.BlockSpec((1, 1, LANES), per_b),
                  pl.BlockSpec((1, 1, width), per_b),
                  pl.BlockSpec((1, 1, width), per_b),
                  pl.BlockSpec((1, 1, LANES), per_b),
                  pl.BlockSpec(memory_space=pltpu.SMEM)],
        out_specs=pl.BlockSpec((1, 1, width), per_b),
        scratch_shapes=[pltpu.VMEM((2, n_slot * n_heads, HEAD_DIM), F32),
                        pltpu.VMEM((2, n_slot * n_heads, HEAD_DIM), F32),
                        pltpu.SemaphoreType.DMA((2, 2))],
    )
    return pl.pallas_call(
        kern,
        grid_spec=grid_spec,
        out_shape=jax.ShapeDtypeStruct((nb, 1, width), BF16),
        compiler_params=_params("arbitrary"),
        name="sample_attention",
    )(page_table, idx, q, cache_k, cache_v, idx.reshape(nb, 1, n_slot), nkeep.reshape(nb, 1, LANES),
      k_new, v_new, sel_new.reshape(nb, 1, LANES), rel_bias)


def _pool_kernel(u_ref, prev_ref, w_ref, ps_ref, o_ref, *, tm, pos0):
    i = pl.program_id(1)
    cur = u_ref[0]
    prev = jnp.where(i == 0, 0.0, prev_ref[0])
    ext = jnp.concatenate([prev, cur], axis=0)
    gdim = cur.shape[1] // len(POOL_WINDOWS)
    t = pos0 + i * tm + lax.broadcasted_iota(I32, (tm, 1), 0)
    for g, win in enumerate(POOL_WINDOWS):
        cs = slice(g * gdim, (g + 1) * gdim)
        s = ext[:, cs]
        span = 1
        while span < win:
            s = s[span:] + s[:-span]
            span *= 2
        wsum = s[s.shape[0] - tm:]
        cnt = jnp.minimum(t + 1, win).astype(F32)
        pooled = (wsum / cnt - cur[:, cs]).astype(BF16)
        mixed = jnp.dot(pooled, w_ref[g], preferred_element_type=F32)
        o_ref[0, :, cs] = (mixed * ps_ref[:, cs]).astype(o_ref.dtype)


def _pool(u, w_pool, pool_scale, pos0):
    nb, n, width = u.shape
    halo = POOL_CTX + 1
    tm = _tile(n, 512)
    ratio = tm // halo
    kern = functools.partial(_pool_kernel, tm=tm, pos0=pos0)
    return pl.pallas_call(
        kern,
        grid=(nb, n // tm),
        in_specs=[pl.BlockSpec((1, tm, width), lambda b, i: (b, i, 0)),
                  pl.BlockSpec((1, halo, width), lambda b, i: (b, jnp.maximum(i * ratio - 1, 0), 0)),
                  pl.BlockSpec(w_pool.shape, lambda b, i: (0, 0, 0)),
                  pl.BlockSpec((1, width), lambda b, i: (0, 0))],
        out_specs=pl.BlockSpec((1, tm, width), lambda b, i: (b, i, 0)),
        out_shape=jax.ShapeDtypeStruct((nb, n, width), BF16),
        compiler_params=_params("arbitrary", "arbitrary"),
        name="pool",
    )(u, u, w_pool, pool_scale)


def _gateproj_kernel(a_ref, p_ref, wa_ref, wp_ref, ga_ref, gb_ref, o_ref):
    a = jnp.dot(a_ref[...], wa_ref[...], preferred_element_type=F32)
    p = jnp.dot(p_ref[...], wp_ref[...], preferred_element_type=F32)
    o_ref[...] = (_sigmoid(ga_ref[...]) * a + _sigmoid(gb_ref[...]) * p).astype(o_ref.dtype)


def _gateproj(attn, pooled, wa, wp, ga, gb):
    m, ka = attn.shape
    kp = pooled.shape[1]
    n = wa.shape[1]
    tm = _tile(m, 1024)
    tn = _tile(n, 512)
    return pl.pallas_call(
        _gateproj_kernel,
        grid=(m // tm, n // tn),
        in_specs=[pl.BlockSpec((tm, ka), lambda i, j: (i, 0)),
                  pl.BlockSpec((tm, kp), lambda i, j: (i, 0)),
                  pl.BlockSpec((ka, tn), lambda i, j: (0, j)),
                  pl.BlockSpec((kp, tn), lambda i, j: (0, j)),
                  pl.BlockSpec((tm, tn), lambda i, j: (i, j)),
                  pl.BlockSpec((tm, tn), lambda i, j: (i, j))],
        out_specs=pl.BlockSpec((tm, tn), lambda i, j: (i, j)),
        out_shape=jax.ShapeDtypeStruct((m, n), BF16),
        compiler_params=_params("arbitrary", "arbitrary"),
        name="gateproj",
    )(attn, pooled, wa, wp, ga, gb)


def _mixout_kernel(m_ref, w_ref, x_ref, gate_ref, gpost_ref, gpre_ref, sc_ref, sh_ref, x1_ref, h2_ref):
    y = jnp.dot(m_ref[...], w_ref[...], preferred_element_type=F32)
    x1 = x_ref[...] + gate_ref[...] * _rms(y, gpost_ref[...])
    x1_ref[...] = x1
    h2_ref[...] = (_rms(x1, gpre_ref[...]) * (1 + sc_ref[...]) + sh_ref[...]).astype(h2_ref.dtype)


def _mixout(m, w_out, x, gate1, g_post, g_pre_ffn, scale2, shift2):
    t, d = x.shape
    tm = _tile(t, 256)
    row = pl.BlockSpec((1, d), lambda i: (0, 0))
    tok = pl.BlockSpec((tm, d), lambda i: (i, 0))
    return pl.pallas_call(
        _mixout_kernel,
        grid=(t // tm,),
        in_specs=[tok, pl.BlockSpec((d, d), lambda i: (0, 0)), tok, _mod_spec(gate1, tm, d),
                  row, row, _mod_spec(scale2, tm, d), _mod_spec(shift2, tm, d)],
        out_specs=[tok, tok],
        out_shape=[jax.ShapeDtypeStruct((t, d), F32), jax.ShapeDtypeStruct((t, d), BF16)],
        compiler_params=_params("arbitrary"),
        name="mixout",
    )(m, w_out, x, gate1, g_post, g_pre_ffn, scale2, shift2)


def _route(h_ref, wt_ref, b_ref, tm):
    n_exp = wt_ref.shape[0]
    per_group = n_exp // N_GROUP
    logits = lax.dot_general(wt_ref[...], h_ref[...], (((1,), (1,)), ((), ())),
                             preferred_element_type=F32)
    s = _sigmoid(logits)
    sb = s + b_ref[...]
    gidx = lax.broadcasted_iota(I32, (per_group, tm), 0)
    gscores = []
    for g in range(N_GROUP):
        x = sb[g * per_group:(g + 1) * per_group]
        m1 = jnp.max(x, axis=0, keepdims=True)
        i1 = jnp.min(jnp.where(x == m1, gidx, per_group), axis=0, keepdims=True)
        m2 = jnp.max(jnp.where(gidx == i1, -jnp.inf, x), axis=0, keepdims=True)
        gscores.append(m1 + m2)
    cur = jnp.concatenate(gscores, axis=0)
    ridx = lax.broadcasted_iota(I32, (N_GROUP, tm), 0)
    gsel = jnp.zeros((N_GROUP, tm), F32)
    for _ in range(TOPK_GROUP):
        m = jnp.max(cur, axis=0, keepdims=True)
        im = jnp.min(jnp.where(cur == m, ridx, N_GROUP), axis=0, keepdims=True)
        pick = ridx == im
        gsel = jnp.where(pick, 1.0, gsel)
        cur = jnp.where(pick, -jnp.inf, cur)
    emask = jnp.concatenate(
        [jnp.broadcast_to(gsel[g:g + 1], (per_group, tm)) for g in range(N_GROUP)], axis=0)
    cur = jnp.where(emask > 0.0, sb, -jnp.inf)
    eidx = lax.broadcasted_iota(I32, (n_exp, tm), 0)
    esel = jnp.zeros((n_exp, tm), F32)
    picks = []
    for _ in range(TOP_K):
        m = jnp.max(cur, axis=0, keepdims=True)
        im = jnp.min(jnp.where(cur == m, eidx, n_exp), axis=0, keepdims=True)
        pick = eidx == im
        esel = jnp.where(pick, 1.0, esel)
        cur = jnp.where(pick, -jnp.inf, cur)
        picks.append(im)
    w = jnp.where(esel > 0.0, s, 0.0)
    gates = w / jnp.sum(w, axis=0, keepdims=True) * ROUTED_SCALE
    return gates, esel, picks


def _router_kernel(h_ref, wt_ref, b_ref, o_ref):
    gates, _, _ = _route(h_ref, wt_ref, b_ref, o_ref.shape[1])
    o_ref[...] = gates


def _router(h, w_router_t, b_router):
    t, d = h.shape
    n_exp = w_router_t.shape[0]
    tm = _tile(t, 512)
    return pl.pallas_call(
        _router_kernel,
        grid=(t // tm,),
        in_specs=[pl.BlockSpec((tm, d), lambda i: (i, 0)),
                  pl.BlockSpec((n_exp, d), lambda i: (0, 0)),
                  pl.BlockSpec((n_exp, 1), lambda i: (0, 0))],
        out_specs=pl.BlockSpec((n_exp, tm), lambda i: (0, i)),
        out_shape=jax.ShapeDtypeStruct((n_exp, t), F32),
        compiler_params=_params("arbitrary"),
        name="router",
    )(h, w_router_t, b_router)


def _router_dispatch_kernel(h_ref, wt_ref, b_ref, eid_ref, pos_ref, wt_out_ref, cnt_ref):
    tm = eid_ref.shape[1]
    n_exp = wt_ref.shape[0]
    gates, esel, picks = _route(h_ref, wt_ref, b_ref, tm)

    @pl.when(pl.program_id(0) == 0)
    def _():
        cnt_ref[...] = jnp.zeros(cnt_ref.shape, F32)

    upper = (lax.broadcasted_iota(I32, (tm, tm), 0)
             < lax.broadcasted_iota(I32, (tm, tm), 1)).astype(BF16)
    before = cnt_ref[:, 0:1] + jnp.dot(esel.astype(BF16), upper, preferred_element_type=F32)
    eidx = lax.broadcasted_iota(I32, (n_exp, tm), 0)
    for k, im in enumerate(picks):
        hit = eidx == im
        eid_ref[k:k + 1, :] = im
        pos_ref[k:k + 1, :] = jnp.sum(jnp.where(hit, before, 0.0), axis=0, keepdims=True).astype(I32)
        wt_out_ref[k:k + 1, :] = jnp.sum(jnp.where(hit, gates, 0.0), axis=0, keepdims=True)
    cnt_ref[...] = cnt_ref[...] + jnp.sum(esel, axis=1, keepdims=True)


def _router_dispatch(h, w_router_t, b_router):
    t, d = h.shape
    n_exp = w_router_t.shape[0]
    tm = _tile(t, 512)
    pick_spec = pl.BlockSpec((TOP_K, tm), lambda i: (0, i))
    return pl.pallas_call(
        _router_dispatch_kernel,
        grid=(t // tm,),
        in_specs=[pl.BlockSpec((tm, d), lambda i: (i, 0)),
                  pl.BlockSpec((n_exp, d), lambda i: (0, 0)),
                  pl.BlockSpec((n_exp, 1), lambda i: (0, 0))],
        out_specs=[pick_spec, pick_spec, pick_spec, pl.BlockSpec((n_exp, LANES), lambda i: (0, 0))],
        out_shape=[jax.ShapeDtypeStruct((TOP_K, t), I32), jax.ShapeDtypeStruct((TOP_K, t), I32),
                   jax.ShapeDtypeStruct((TOP_K, t), F32), jax.ShapeDtypeStruct((n_exp, LANES), F32)],
        compiler_params=_params("arbitrary"),
        name="router_dispatch",
    )(h, w_router_t, b_router)


def _moe_kernel(h_ref, g_ref, wg_ref, wu_ref, wd_ref, init_ref, x_ref, gate_ref, gpost_ref,
                o_ref, acc_ref, *, final):
    e = pl.program_id(1)
    n_exp = pl.num_programs(1)

    @pl.when(e == 0)
    def _():
        acc_ref[...] = init_ref[...]

    gates = g_ref[...]
    lane = lax.broadcasted_iota(I32, gates.shape, 1)
    gcol = jnp.sum(jnp.where(lane == e, gates, 0.0), axis=1, keepdims=True)

    @pl.when(jnp.max(jnp.abs(gcol)) > 0.0)
    def _():
        hh = h_ref[...]
        a = jnp.dot(hh, wg_ref[0], preferred_element_type=F32)
        b = jnp.dot(hh, wu_ref[0], preferred_element_type=F32)
        act = (a * _sigmoid(a) * b).astype(BF16)
        y = jnp.dot(act, wd_ref[0], preferred_element_type=F32)
        acc_ref[...] += y * gcol

    @pl.when(e == n_exp - 1)
    def _():
        if final:
            o_ref[...] = x_ref[...] + gate_ref[...] * _rms(acc_ref[...], gpost_ref[...])
        else:
            o_ref[...] = acc_ref[...]


def _moe(h, gates, wg, wu, wd, init, x, gate2, g_post, final):
    t, d = h.shape
    n_exp, _, f = wg.shape
    tm = _tile(t, 512)
    tok = pl.BlockSpec((tm, d), lambda i, e: (i, 0))
    kern = functools.partial(_moe_kernel, final=final)
    return pl.pallas_call(
        kern,
        grid=(t // tm, n_exp),
        in_specs=[tok,
                  pl.BlockSpec((tm, gates.shape[1]), lambda i, e: (i, 0)),
                  pl.BlockSpec((1, d, f), lambda i, e: (e, 0, 0)),
                  pl.BlockSpec((1, d, f), lambda i, e: (e, 0, 0)),
                  pl.BlockSpec((1, f, d), lambda i, e: (e, 0, 0)),
                  tok, tok, _mod_spec(gate2, tm, d),
                  pl.BlockSpec((1, d), lambda i, e: (0, 0))],
        out_specs=tok,
        out_shape=jax.ShapeDtypeStruct((t, d), F32),
        scratch_shapes=[pltpu.VMEM((tm, d), F32)],
        compiler_params=_params("arbitrary", "arbitrary"),
        name="moe_final" if final else "moe_shared",
    )(h, gates, wg, wu, wd, init, x, gate2, g_post)


def _slots_kernel(pstart_ref, eid_ref, pos_ref, o_ref):
    eid = eid_ref[...]
    base = jnp.zeros(eid.shape, I32)
    for e in range(pstart_ref.shape[0]):
        base = jnp.where(eid == e, pstart_ref[e], base)
    o_ref[...] = base + pos_ref[...]


def _slots(pstart, eid, pos):
    return pl.pallas_call(
        _slots_kernel,
        in_specs=[pl.BlockSpec(memory_space=pltpu.SMEM),
                  pl.BlockSpec(memory_space=pltpu.VMEM), pl.BlockSpec(memory_space=pltpu.VMEM)],
        out_shape=jax.ShapeDtypeStruct(eid.shape, I32),
        name="slots",
    )(pstart, eid, pos)


def _dispatch_kernel(slot_hbm, h_ref, xs_in, xs_out, slot_smem, sem_ref, *, tm, rt):
    del xs_in
    i = pl.program_id(0)
    t0 = pl.multiple_of(i * tm, tm)
    idx_copy = pltpu.make_async_copy(slot_hbm.at[:, pl.ds(t0, tm)], slot_smem, sem_ref.at[0])
    idx_copy.start()
    idx_copy.wait()

    def row_copy(k, t):
        src = h_ref.at[pl.ds(t * rt, rt)]
        dst = xs_out.at[pl.ds(slot_smem[k, t] * rt, rt)]
        return pltpu.make_async_copy(src, dst, sem_ref.at[1])

    def start(t, carry):
        for k in range(TOP_K):
            row_copy(k, t).start(priority=k % 2)
        return carry
    lax.fori_loop(0, tm, start, 0)

    def wait(t, carry):
        for k in range(TOP_K):
            row_copy(k, t).wait()
        return carry
    lax.fori_loop(0, tm, wait, 0)


def _dispatch(slot, h_rows, n_slots):
    t = slot.shape[1]
    rt = h_rows.shape[0] // t
    tm = _tile(t, 256)
    xs0 = jnp.zeros((n_slots * rt, LANES), F32)
    kern = functools.partial(_dispatch_kernel, tm=tm, rt=rt)
    return pl.pallas_call(
        kern,
        grid=(t // tm,),
        in_specs=[pl.BlockSpec(memory_space=pl.ANY),
                  pl.BlockSpec((tm * rt, LANES), lambda i: (i, 0)),
                  pl.BlockSpec(memory_space=pl.ANY)],
        out_specs=pl.BlockSpec(memory_space=pl.ANY),
        out_shape=jax.ShapeDtypeStruct(xs0.shape, F32),
        scratch_shapes=[pltpu.SMEM((TOP_K, tm), I32), pltpu.SemaphoreType.DMA((2,))],
        input_output_aliases={2: 0},
        compiler_params=_params("arbitrary"),
        name="dispatch",
    )(slot, h_rows, xs0)


def _expert_kernel(te_ref, nu_ref, x_ref, wg_ref, wu_ref, wd_ref, o_ref, *, ts, rt):
    s = pl.program_id(0)

    @pl.when(s < nu_ref[0])
    def _():
        x = jnp.concatenate([x_ref[pl.ds(j, ts, stride=rt), :] for j in range(rt)],
                            axis=1).astype(BF16)
        a = jnp.dot(x, wg_ref[0], preferred_element_type=F32)
        b = jnp.dot(x, wu_ref[0], preferred_element_type=F32)
        act = (a * _sigmoid(a) * b).astype(BF16)
        y = jnp.dot(act, wd_ref[0], preferred_element_type=F32)
        for j in range(rt):
            o_ref[pl.ds(j, ts, stride=rt), :] = y[:, j * LANES:(j + 1) * LANES]

    @pl.when(s >= nu_ref[0])
    def _():
        o_ref[...] = jnp.zeros(o_ref.shape, F32)


def _experts(tile_e, n_used, xs, wg, wu, wd, ts):
    n_exp, d, f = wg.shape
    n_tiles = tile_e.shape[0]
    rt = d // LANES
    kern = functools.partial(_expert_kernel, ts=ts, rt=rt)
    grid_spec = pltpu.PrefetchScalarGridSpec(
        num_scalar_prefetch=2,
        grid=(n_tiles,),
        in_specs=[pl.BlockSpec((ts * rt, LANES), lambda s, te, nu: (s, 0)),
                  pl.BlockSpec((1, d, f), lambda s, te, nu: (te[s], 0, 0)),
                  pl.BlockSpec((1, d, f), lambda s, te, nu: (te[s], 0, 0)),
                  pl.BlockSpec((1, f, d), lambda s, te, nu: (te[s], 0, 0))],
        out_specs=pl.BlockSpec((ts * rt, LANES), lambda s, te, nu: (s, 0)),
    )
    return pl.pallas_call(
        kern,
        grid_spec=grid_spec,
        out_shape=jax.ShapeDtypeStruct(xs.shape, F32),
        compiler_params=_params("arbitrary"),
        name="experts",
    )(tile_e, n_used, xs, wg, wu, wd)


def _combine_kernel(slot_hbm, ys_hbm, wt_ref, init_ref, x_ref, gate_ref, gpost_ref, o_ref,
                    slot_smem, buf_ref, sem_ref, *, tm, rt):
    i = pl.program_id(0)
    t0 = pl.multiple_of(i * tm, tm)
    idx_copy = pltpu.make_async_copy(slot_hbm.at[:, pl.ds(t0, tm)], slot_smem, sem_ref.at[0])
    idx_copy.start()
    idx_copy.wait()

    def row_copy(k, t):
        src = ys_hbm.at[pl.ds(slot_smem[k, t] * rt, rt)]
        dst = buf_ref.at[pl.ds((k * tm + t) * rt, rt)]
        return pltpu.make_async_copy(src, dst, sem_ref.at[1])

    def start(t, carry):
        for k in range(TOP_K):
            row_copy(k, t).start(priority=k % 2)
        return carry
    lax.fori_loop(0, tm, start, 0)

    def wait(t, carry):
        for k in range(TOP_K):
            row_copy(k, t).wait()
        return carry
    lax.fori_loop(0, tm, wait, 0)

    wts = wt_ref[...]
    cols = []
    for j in range(rt):
        acc = init_ref[:, j * LANES:(j + 1) * LANES]
        for k in range(TOP_K):
            part = buf_ref[pl.ds(k * tm * rt + j, tm, stride=rt), :]
            acc = acc + part * wts[:, k:k + 1]
        cols.append(acc)
    y = jnp.concatenate(cols, axis=1)
    o_ref[...] = x_ref[...] + gate_ref[...] * _rms(y, gpost_ref[...])


def _combine(slot, ys, wts, init, x, gate2, g_post):
    t, d = x.shape
    tm = _tile(t, 128)
    tok = pl.BlockSpec((tm, d), lambda i: (i, 0))
    rt = d // LANES
    kern = functools.partial(_combine_kernel, tm=tm, rt=rt)
    return pl.pallas_call(
        kern,
        grid=(t // tm,),
        in_specs=[pl.BlockSpec(memory_space=pl.ANY), pl.BlockSpec(memory_space=pl.ANY),
                  pl.BlockSpec((tm, TOP_K), lambda i: (i, 0)), tok, tok, _mod_spec(gate2, tm, d),
                  pl.BlockSpec((1, d), lambda i: (0, 0))],
        out_specs=tok,
        out_shape=jax.ShapeDtypeStruct((t, d), F32),
        scratch_shapes=[pltpu.SMEM((TOP_K, tm), I32),
                        pltpu.VMEM((TOP_K * tm * rt, LANES), F32),
                        pltpu.SemaphoreType.DMA((2,))],
        compiler_params=_params("arbitrary"),
        name="combine",
    )(slot, ys, wts, init, x, gate2, g_post)


def _routed_by_dispatch(h2, x1, shared, g2, wts):
    t, d = h2.shape
    rt = d // LANES
    n_exp = wts["w_router_t"].shape[0]
    ts = 256
    eid, pos, pick_w, counts = _router_dispatch(h2, wts["w_router_t"], wts["b_router"])
    cnt = counts[:, 0].astype(I32)
    padded = (cnt + ts - 1) // ts * ts
    pend = jnp.cumsum(padded)
    pstart = pend - padded
    n_tiles = (t * TOP_K) // ts + n_exp
    tile_start = jnp.arange(n_tiles, dtype=I32) * ts
    tile_e = jnp.minimum(jnp.sum((pend[None, :] <= tile_start[:, None]).astype(I32), axis=1), n_exp - 1)
    n_used = (pend[-1:] // ts).astype(I32)
    slot = _slots(pstart, eid, pos)
    h_rows = h2.astype(F32).reshape(t * rt, LANES)
    xs = _dispatch(slot, h_rows, n_tiles * ts)
    ys = _experts(tile_e, n_used, xs, wts["w_gate_e"], wts["w_up_e"], wts["w_down_e"], ts)
    return _combine(slot, ys, pick_w.T, shared, x1, g2, wts["g_post_ffn"])


def _token_path(x, mods, wts):
    sh1, sc1 = mods[0], mods[1]
    h = _normmod(x, wts["g_pre_mix"], sc1, sh1)
    q = _mm(h, wts["w_q"], BF16)
    k = _mm(h, wts["w_k"], F32)
    v = _mm(h, wts["w_v"], F32)
    iq = _mm(h, wts["w_iq"], BF16)
    u = _mm(h, wts["w_u"], F32)
    ga = _mm(h, wts["w_ga"], F32)
    gb = _mm(h, wts["w_gb"], F32)
    ik, iw = _idxproj(h, wts["w_idx"], wts["idx_k_norm"])
    return q, k, v, iq, iw, ik, u, ga, gb


def _back_half(x, attn, pooled, ga, gb, mods, wts, dispatch):
    _, _, g1, sh2, sc2, g2 = mods
    m = _gateproj(attn, pooled, wts["w_proj_attn"], wts["w_proj_pool"], ga, gb)
    x1, h2 = _mixout(m, wts["w_out"], x, g1, wts["g_post_mix"], wts["g_pre_ffn"], sc2, sh2)
    t, d = x.shape
    ones = jnp.ones((t, 1), F32)
    zeros = jnp.zeros((t, d), F32)
    shared = _moe(h2, ones, wts["w_gate_s"], wts["w_up_s"], wts["w_down_s"], zeros, x1, g2,
                  wts["g_post_ffn"], final=False)
    if dispatch:
        return _routed_by_dispatch(h2, x1, shared, g2, wts)
    gates = _router(h2, wts["w_router_t"], wts["b_router"]).T
    return _moe(h2, gates, wts["w_gate_e"], wts["w_up_e"], wts["w_down_e"], shared, x1, g2,
                wts["g_post_ffn"], final=True)


def kernel(x_prompt, x_sample, c_prompt, c_sample, cache_k, cache_v, cache_idx_k, state_pool, page_table, w_ada, b_ada, g_pre_mix, w_in, idx_k_norm, rel_bias, w_pool, pool_scale, w_proj_attn, w_proj_pool, w_out, g_post_mix, g_pre_ffn, w_router, b_router, w_gate_e, w_up_e, w_down_e, w_gate_s, w_up_s, w_down_s, g_post_ffn):
    depth = w_ada.shape[0]
    assert depth == 1, "single-layer trunk"
    nbp, seq, d = x_prompt.shape
    assert nbp == 1, "one prompt sequence"
    nbs, dec_seq, _ = x_sample.shape
    assert dec_seq == 1, "one new token per sample sequence"
    _, n_phys, page, n_heads, head_dim = cache_k.shape
    assert head_dim == HEAD_DIM and page == LANES
    aw = n_heads * HEAD_DIM
    pw = state_pool.shape[-1]
    n_pages = page_table.shape[1]
    past = n_pages * page
    iqw = N_IDX_HEADS * IDX_DIM

    win = w_in[0]
    offs = np.cumsum([0, aw, aw, aw, iqw, IDX_DIM, N_IDX_HEADS, pw, d, d])
    seg = lambda a: win[:, offs[a]:offs[a + 1]].astype(BF16)
    w_idx = jnp.pad(win[:, offs[4]:offs[6]], ((0, 0), (0, LANES - IDX_DIM - N_IDX_HEADS))).astype(BF16)
    wts = dict(
        g_pre_mix=g_pre_mix, idx_k_norm=idx_k_norm,
        w_q=seg(0), w_k=seg(1), w_v=seg(2), w_iq=seg(3), w_idx=w_idx, w_u=seg(6), w_ga=seg(7), w_gb=seg(8),
        w_proj_attn=w_proj_attn[0].astype(BF16), w_proj_pool=w_proj_pool[0].astype(BF16),
        w_out=w_out[0].astype(BF16), g_post_mix=g_post_mix, g_pre_ffn=g_pre_ffn,
        w_router_t=w_router[0].T.astype(BF16), b_router=b_router.reshape(-1, 1),
        w_gate_e=w_gate_e[0].astype(BF16), w_up_e=w_up_e[0].astype(BF16), w_down_e=w_down_e[0].astype(BF16),
        w_gate_s=w_gate_s.astype(BF16), w_up_s=w_up_s.astype(BF16), w_down_s=w_down_s.astype(BF16),
        g_post_ffn=g_post_ffn,
    )
    w_pool_b = w_pool[0].astype(BF16)

    n_c = nbp + nbs
    rows = -(-n_c // SUBLANES) * SUBLANES
    c_all = jnp.pad(jnp.concatenate([c_prompt, c_sample], axis=0), ((0, rows - n_c), (0, 0)))
    mod = _ada(c_all, w_ada[0], b_ada)
    mods_p = [mod[0:1, a * d:(a + 1) * d] for a in range(6)]
    mods_s = [mod[nbp:n_c, a * d:(a + 1) * d] for a in range(6)]

    bias_tiles = _bias_tiles(rel_bias)

    xp = x_prompt.reshape(seq, d)
    q, k, v, iq, iw, ik, u, ga, gb = _token_path(xp, mods_p, wts)
    mask = _prompt_select(iq, ik.astype(BF16).T, iw, min(TOPK_MAX, seq // 4))
    attn = _prompt_attention(q, k.astype(BF16), v.astype(BF16), mask, bias_tiles)
    pooled = _pool(u.reshape(1, seq, pw), w_pool_b, pool_scale, 0).reshape(seq, pw)
    y_prompt = _back_half(xp, attn, pooled, ga, gb, mods_p, wts, True).reshape(nbp, seq, d)
    k_prompt = k.reshape(1, nbp, seq, n_heads, HEAD_DIM)
    v_prompt = v.reshape(1, nbp, seq, n_heads, HEAD_DIM)
    idxk_prompt = ik.reshape(1, nbp, seq, IDX_DIM)
    pool_prompt = u[seq - POOL_CTX:].reshape(1, nbp, POOL_CTX, pw)

    xs = x_sample.reshape(nbs, d)
    qs, ks, vs, iqs, iws, iks, us, gas, gbs = _token_path(xs, mods_s, wts)
    iq3 = iqs.reshape(nbs, N_IDX_HEADS, IDX_DIM)
    iw3 = iws.reshape(nbs, N_IDX_HEADS, 1)
    scores = _sample_scores(iq3, iw3, cache_idx_k, page_table).reshape(nbs, past)
    n_sel_s = min(TOPK_MAX, (past + dec_seq) // 4)
    n_slot = -(-n_sel_s // LANES) * LANES
    idx, nkeep, sel_new = _sample_select(scores, iq3, iws, iks.astype(BF16).reshape(nbs, 1, IDX_DIM),
                                         n_sel_s, n_slot)
    attn_s = _sample_attention(
        qs.reshape(nbs, 1, aw), ks.reshape(nbs, 1, aw), vs.reshape(nbs, 1, aw), cache_k, cache_v,
        idx.reshape(nbs, n_slot), nkeep, sel_new, page_table, rel_bias).reshape(nbs, aw)
    ext = jnp.concatenate([state_pool[0], us.reshape(nbs, 1, pw)], axis=1)
    pooled_s = _pool(ext, w_pool_b, pool_scale, past - POOL_CTX)[:, POOL_CTX]
    y_sample = _back_half(xs, attn_s, pooled_s, gas, gbs, mods_s, wts, False).reshape(nbs, 1, d)
    k_sample = ks.reshape(1, nbs, 1, n_heads, HEAD_DIM)
    v_sample = vs.reshape(1, nbs, 1, n_heads, HEAD_DIM)
    idxk_sample = iks.reshape(1, nbs, 1, IDX_DIM)
    pool_sample = ext[:, 1:].reshape(1, nbs, POOL_CTX, pw)

    return (y_prompt, y_sample, k_prompt, v_prompt, idxk_prompt, pool_prompt,
            k_sample, v_sample, idxk_sample, pool_sample)
```
